```python
import math
import jax, jax.numpy as jnp
from jax import lax
import numpy as np

D_MODEL = 1024
BATCH = 8
SEQ = 2048
DEPTH = 4

GRID_W = 64
CTX_LEN = 256

ATT_HEADS = 8
QK_DIM = 64
V_DIM = 2 * QK_DIM
Q_W = ATT_HEADS * 2 * QK_DIM
ATT_W = ATT_HEADS * V_DIM
Q_BLOCK = 128
ROPE_FREQS = QK_DIM // 4
ROPE_BASE = 10000.0
FOURIER_GROUPS = 4
FOURIER_GC = 128
FOURIER_W = FOURIER_GROUPS * FOURIER_GC
POOL_WINDOWS = (2, 4, 8, 16)
POOL_GROUPS = len(POOL_WINDOWS)
POOL_GC = 128
POOL_W = POOL_GROUPS * POOL_GC
N_BRANCH = 3
GATE_W = N_BRANCH * D_MODEL
OFF_K = 0
OFF_V = OFF_K + Q_W
OFF_Q = OFF_V + ATT_W
OFF_F = OFF_Q + Q_W
OFF_P = OFF_F + FOURIER_W
OFF_G = OFF_P + POOL_W
N_IN = OFF_G + GATE_W
D_FF = -(-8 * D_MODEL // (3 * 256)) * 256
ALPHA = (2 * DEPTH) ** 0.25
BETA = (8 * DEPTH) ** -0.25
LN_EPS = 1e-5

kernel_name = "hybrid_diffattn_fourier_pool_dit"


def layer_norm(x, g=None, b=None):
    xf = x.astype(jnp.float32)
    mu = jnp.mean(xf, axis=-1, keepdims=True)
    var = jnp.mean(jnp.square(xf - mu), axis=-1, keepdims=True)
    y = (xf - mu) * lax.rsqrt(var + LN_EPS)
    if g is not None:
        y = y * g.astype(jnp.float32) + b.astype(jnp.float32)
    return y.astype(x.dtype)


def modulate(h, shift, scale):
    return h * (1 + scale) + shift


def axial_rope_tables(rows, dtype):
    row = jnp.repeat(jnp.arange(rows), GRID_W).astype(jnp.float32)
    col = jnp.tile(jnp.arange(GRID_W), rows).astype(jnp.float32)
    inv = ROPE_BASE ** (-jnp.arange(ROPE_FREQS, dtype=jnp.float32) / ROPE_FREQS)
    ar = row[:, None] * inv[None, :]
    ac = col[:, None] * inv[None, :]
    ang = jnp.concatenate([ar, ar, ac, ac], axis=-1)
    return jnp.cos(ang).astype(dtype), jnp.sin(ang).astype(dtype)


def apply_rope(x, cos, sin):
    xr = x.reshape(x.shape[:-1] + (2, 2, ROPE_FREQS))
    rot = jnp.stack([-xr[..., 1, :], xr[..., 0, :]], axis=-2).reshape(x.shape)
    return x * cos[:, None, :] + rot * sin[:, None, :]


def diff_attend(q1, q2, k1, k2, v, lam):
    scale = QK_DIM ** -0.5
    s1 = jnp.einsum('bqhd,bkhd->bhqk', q1, k1).astype(jnp.float32) * scale
    s2 = jnp.einsum('bqhd,bkhd->bhqk', q2, k2).astype(jnp.float32) * scale
    a = jax.nn.softmax(s1, axis=-1) - lam * jax.nn.softmax(s2, axis=-1)
    return jnp.einsum('bhqk,bkhv->bqhv', a.astype(v.dtype), v)


def blocked_diff_attention(q1, q2, k1, k2, v, lam):
    B, S, H, _ = q1.shape
    nb = S // Q_BLOCK

    def blocks(t):
        return jnp.moveaxis(t.reshape(B, nb, Q_BLOCK, H, t.shape[-1]), 1, 0)

    out = lax.map(lambda qs: diff_attend(qs[0], qs[1], k1, k2, v, lam), (blocks(q1), blocks(q2)))
    return jnp.moveaxis(out, 0, 1).reshape(B, S, H, v.shape[-1])


def diff_head_norm(o, g, lam_init):
    B, L = o.shape[:2]
    of = o.astype(jnp.float32)
    y = of * lax.rsqrt(jnp.mean(jnp.square(of), axis=-1, keepdims=True) + LN_EPS)
    y = y * g.astype(jnp.float32) * (1.0 - lam_init)
    return y.reshape(B, L, ATT_W).astype(o.dtype)


def fourier_mix(h):
    B, L, _ = h.shape
    hg = h.reshape(B, L, FOURIER_GROUPS, FOURIER_GC).astype(jnp.float32)
    y = jnp.fft.fft2(hg, axes=(1, 3), norm='ortho').real
    return y.reshape(B, L, FOURIER_W).astype(h.dtype)


def multiscale_pool(h, w_grp, scale):
    B, L, _ = h.shape
    hg = h.reshape(B, L, POOL_GROUPS, POOL_GC).astype(jnp.float32)
    cs = jnp.concatenate([jnp.zeros_like(hg[:, :1]), jnp.cumsum(hg, axis=1)], axis=1)
    t = jnp.arange(L)
    means = []
    for g, w in enumerate(POOL_WINDOWS):
        lo = w // 2
        hi = w - lo
        start = jnp.clip(t - lo, 0, L)
        end = jnp.clip(t + hi, 0, L)
        cnt = (end - start).astype(jnp.float32)
        means.append((cs[:, end, g] - cs[:, start, g]) / cnt[None, :, None])
    pooled = jnp.stack(means, axis=2)
    y = jnp.einsum('blgc,gcd->blgd', (pooled - hg).astype(h.dtype), w_grp)
    return y.reshape(B, L, POOL_W) * scale


def mixer_output(att, tail, lam_init, subln_g_l, w_att_br_l, w_four_br_l,
                 w_pool_grp_l, pool_scale_l, w_pool_br_l, w_out_l):
    B, L = att.shape[:2]
    f_in = tail[..., :FOURIER_W]
    p_in = tail[..., FOURIER_W:FOURIER_W + POOL_W]
    gates = tail[..., FOURIER_W + POOL_W:]
    b_att = diff_head_norm(att, subln_g_l, lam_init) @ w_att_br_l
    b_four = fourier_mix(f_in) @ w_four_br_l
    b_pool = multiscale_pool(p_in, w_pool_grp_l, pool_scale_l) @ w_pool_br_l
    g = jax.nn.sigmoid(gates.astype(jnp.float32)).astype(att.dtype).reshape(B, L, N_BRANCH, D_MODEL)
    m = g[..., 0, :] * b_att + g[..., 1, :] * b_four + g[..., 2, :] * b_pool
    return m @ w_out_l


def swiglu(h, wg, wu, wd):
    return (jax.nn.silu(h @ wg) * (h @ wu)) @ wd


def setup_inputs(seed: int = 0) -> dict:
    key = jax.random.key(seed)
    ks = jax.random.split(key, 24)

    def nrm(k, shape, s):
        return jax.random.normal(k, shape, jnp.float32) * s

    D = D_MODEL
    return {
        "x": nrm(ks[0], (BATCH, SEQ, D), 1.0),
        "c": nrm(ks[1], (BATCH, D), 1.0),
        "ctx": nrm(ks[2], (BATCH, CTX_LEN, D), 1.0),
        "c_ctx": nrm(ks[3], (D,), 1.0),
        "w_mod": nrm(ks[4], (DEPTH, D, 6 * D), 0.5 * D ** -0.5),
        "b_mod": nrm(ks[5], (DEPTH, 6 * D), 0.01),
        "w_in": nrm(ks[6], (DEPTH, D, N_IN), D ** -0.5),
        "lam_qk": nrm(ks[7], (DEPTH, 4, QK_DIM), 0.1),
        "subln_g": 1.0 + nrm(ks[8], (DEPTH, V_DIM), 0.02),
        "w_att_br": nrm(ks[9], (DEPTH, ATT_W, D), ATT_W ** -0.5),
        "w_four_br": nrm(ks[10], (DEPTH, FOURIER_W, D), FOURIER_W ** -0.5),
        "w_pool_grp": nrm(ks[11], (DEPTH, POOL_GROUPS, POOL_GC, POOL_GC), POOL_GC ** -0.5),
        "pool_scale": 1.0 + nrm(ks[12], (DEPTH, POOL_W), 0.02),
        "w_pool_br": nrm(ks[13], (DEPTH, POOL_W, D), POOL_W ** -0.5),
        "w_out": nrm(ks[14], (DEPTH, D, D), BETA * D ** -0.5),
        "ln1_g": 1.0 + nrm(ks[15], (DEPTH, D), 0.02),
        "ln1_b": nrm(ks[16], (DEPTH, D), 0.01),
        "w_ffn_gate": nrm(ks[17], (DEPTH, D, D_FF), D ** -0.5),
        "w_ffn_up": nrm(ks[18], (DEPTH, D, D_FF), D ** -0.5),
        "w_ffn_down": nrm(ks[19], (DEPTH, D_FF, D), BETA * D_FF ** -0.5),
        "ln2_g": 1.0 + nrm(ks[20], (DEPTH, D), 0.02),
        "ln2_b": nrm(ks[21], (DEPTH, D), 0.01),
    }


def reference(x, c, ctx, c_ctx, w_mod, b_mod, w_in, lam_qk, subln_g, w_att_br, w_four_br,
              w_pool_grp, pool_scale, w_pool_br, w_out, ln1_g, ln1_b, w_ffn_gate, w_ffn_up,
              w_ffn_down, ln2_g, ln2_b):
    B, S, _ = x.shape
    Lc = ctx.shape[1]
    H = ATT_HEADS
    ROWS = S // GRID_W
    cos, sin = axial_rope_tables(ROWS, x.dtype)
    xc = ctx
    silu_c = jax.nn.silu(c)
    silu_cc = jax.nn.silu(c_ctx)

    for l in range(DEPTH):
        last = l == DEPTH - 1
        lam_init = 0.8 - 0.6 * math.exp(-0.3 * l)
        lf = lam_qk[l].astype(jnp.float32)
        lam = jnp.exp(jnp.sum(lf[0] * lf[1])) - jnp.exp(jnp.sum(lf[2] * lf[3])) + lam_init

        mod = (silu_c @ w_mod[l] + b_mod[l])[:, None, :]
        modc = (silu_cc @ w_mod[l] + b_mod[l])[None, None, :]
        sh1, sc1, g1, sh2, sc2, g2 = jnp.split(mod, 6, axis=-1)
        csh1, csc1, cg1, csh2, csc2, cg2 = jnp.split(modc, 6, axis=-1)

        uc = modulate(layer_norm(xc), csh1, csc1)
        pc_kv = uc @ w_in[l][:, :OFF_Q]
        ck = pc_kv[..., OFF_K:OFF_V].reshape(B, Lc, H, 2, QK_DIM)
        cv = pc_kv[..., OFF_V:OFF_Q].reshape(B, Lc, H, V_DIM)

        u = modulate(layer_norm(x), sh1, sc1)
        p = u @ w_in[l]
        k = p[..., OFF_K:OFF_V].reshape(B, S, H, 2, QK_DIM)
        v = p[..., OFF_V:OFF_Q].reshape(B, S, H, V_DIM)
        q = p[..., OFF_Q:OFF_F].reshape(B, S, H, 2, QK_DIM)
        q1 = apply_rope(q[..., 0, :], cos, sin)
        q2 = apply_rope(q[..., 1, :], cos, sin)
        k1 = jnp.concatenate([apply_rope(k[..., 0, :], cos, sin), ck[..., 0, :]], axis=1)
        k2 = jnp.concatenate([apply_rope(k[..., 1, :], cos, sin), ck[..., 1, :]], axis=1)
        vv = jnp.concatenate([v, cv], axis=1)
        att = blocked_diff_attention(q1, q2, k1, k2, vv, lam)
        y = mixer_output(att, p[..., OFF_F:], lam_init, subln_g[l], w_att_br[l], w_four_br[l],
                         w_pool_grp[l], pool_scale[l], w_pool_br[l], w_out[l])
        x = layer_norm(ALPHA * x + g1 * y, ln1_g[l], ln1_b[l])

        if not last:
            pc = uc @ w_in[l][:, OFF_Q:]
            cq = pc[..., :Q_W].reshape(B, Lc, H, 2, QK_DIM)
            catt = diff_attend(cq[..., 0, :], cq[..., 1, :], ck[..., 0, :], ck[..., 1, :], cv, lam)
            yc = mixer_output(catt, pc[..., Q_W:], lam_init, subln_g[l], w_att_br[l], w_four_br[l],
                              w_pool_grp[l], pool_scale[l], w_pool_br[l], w_out[l])
            xc = layer_norm(ALPHA * xc + cg1 * yc, ln1_g[l], ln1_b[l])

        f = swiglu(modulate(layer_norm(x), sh2, sc2), w_ffn_gate[l], w_ffn_up[l], w_ffn_down[l])
        x = layer_norm(ALPHA * x + g2 * f, ln2_g[l], ln2_b[l])
        if not last:
            fc = swiglu(modulate(layer_norm(xc), csh2, csc2), w_ffn_gate[l], w_ffn_up[l], w_ffn_down[l])
            xc = layer_norm(ALPHA * xc + cg2 * fc, ln2_g[l], ln2_b[l])

    return x
```

```python
import functools
import math

import jax
import jax.numpy as jnp
import numpy as np
from jax import lax
from jax.experimental import pallas as pl
from jax.experimental.pallas import tpu as pltpu

F32 = jnp.float32
BF16 = jnp.bfloat16

D_MODEL = 1024
GRID_W = 64
ATT_HEADS = 8
QK_DIM = 64
V_DIM = 128
ROPE_FREQS = 16
ROPE_BASE = 10000.0
FOURIER_GROUPS = 4
GROUP_C = 128
POOL_WINDOWS = (2, 4, 8, 16)
OFF_K, OFF_V, OFF_Q, OFF_F, OFF_P, OFF_G = 0, 1024, 2048, 3072, 3584, 4096
N_IN = 7168
LN_EPS = 1e-5
LOG2_E = math.log2(math.e)

LANES = 128
VMEM_LIMIT = 56 * 1024 * 1024

TM_IN = 1024
TN_IN = 1024
ATT_TQ = 256
TM_MIX = 512
TM_FFN = 512
TR_FOUR = 512
POOL_T = 256
POOL_PAD = 128
MOD_ROWS = 16


def _cparams(sem):
    return pltpu.CompilerParams(dimension_semantics=sem, vmem_limit_bytes=VMEM_LIMIT)


def _layer_norm(x):
    mu = jnp.mean(x, axis=-1, keepdims=True)
    xc = x - mu
    var = jnp.mean(xc * xc, axis=-1, keepdims=True)
    return xc * lax.rsqrt(var + LN_EPS)


def _sigmoid(x):
    return 1.0 / (1.0 + jnp.exp(-x))


def _mod_kernel(c_ref, w_ref, b_ref, o_ref):
    c = c_ref[...]
    sc = (c * _sigmoid(c)).astype(BF16)
    o_ref[...] = jnp.dot(sc, w_ref[...].astype(BF16), preferred_element_type=F32) + b_ref[...]


def _mod_vectors(cond, w_mod, b_mod):
    depth, d, n = w_mod.shape
    tn = 1024
    return pl.pallas_call(
        _mod_kernel,
        grid=(depth, n // tn),
        in_specs=[
            pl.BlockSpec((MOD_ROWS, d), lambda l, j: (0, 0)),
            pl.BlockSpec((None, d, tn), lambda l, j: (l, 0, j)),
            pl.BlockSpec((None, 1, tn), lambda l, j: (l, 0, j)),
        ],
        out_specs=pl.BlockSpec((None, MOD_ROWS, tn), lambda l, j: (l, 0, j)),
        out_shape=jax.ShapeDtypeStruct((depth, MOD_ROWS, n), F32),
        compiler_params=_cparams(("arbitrary", "arbitrary")),
        name="mod_vectors",
    )(cond, w_mod, b_mod.reshape(depth, 1, n))


def _inproj_kernel(x_ref, sh_ref, sc_ref, w_ref, cos_ref, sa_ref, sb_ref, o_ref, u_ref):
    j = pl.program_id(1)

    @pl.when(j == 0)
    def _():
        h = _layer_norm(x_ref[...])
        u_ref[...] = (h * (1.0 + sc_ref[...]) + sh_ref[...]).astype(BF16)

    acc = jnp.dot(u_ref[...], w_ref[...], preferred_element_type=F32)
    is_k = j == OFF_K // TN_IN
    is_q = j == OFF_Q // TN_IN
    is_gate = j >= OFF_G // TN_IN

    @pl.when(is_k | is_q)
    def _():
        scale = jnp.where(is_q, QK_DIM ** -0.5 * LOG2_E, 1.0).astype(F32)
        cos, sa, sb = cos_ref[...], sa_ref[...], sb_ref[...]
        for c in range(TN_IN // LANES):
            xs = acc[:, c * LANES:(c + 1) * LANES]
            r = (xs * cos + pltpu.roll(xs, LANES - ROPE_FREQS, 1) * sa
                 + pltpu.roll(xs, ROPE_FREQS, 1) * sb)
            o_ref[:, c * LANES:(c + 1) * LANES] = (r * scale).astype(BF16)

    @pl.when(is_gate)
    def _():
        o_ref[...] = _sigmoid(acc).astype(BF16)

    @pl.when(jnp.logical_not(is_k | is_q | is_gate))
    def _():
        o_ref[...] = acc.astype(BF16)


def _in_projection(x, mod, w_in, rope, layer, n_lat_tiles, tiles_per_batch, n_batch):
    rows, d = x.shape
    cos, sa, sb = rope
    n_rope_blocks = cos.shape[0] // TM_IN - 1

    def mod_idx(i):
        return layer * MOD_ROWS + jnp.minimum(i // tiles_per_batch, n_batch)

    def rope_idx(i, j):
        return (jnp.where(i < n_lat_tiles, i % tiles_per_batch, n_rope_blocks), 0)

    return pl.pallas_call(
        _inproj_kernel,
        grid=(rows // TM_IN, N_IN // TN_IN),
        in_specs=[
            pl.BlockSpec((TM_IN, d), lambda i, j: (i, 0)),
            pl.BlockSpec((None, 1, d), lambda i, j: (mod_idx(i), 0, 0)),
            pl.BlockSpec((None, 1, d), lambda i, j: (mod_idx(i), 0, 1)),
            pl.BlockSpec((None, d, TN_IN), lambda i, j: (layer, 0, j)),
            pl.BlockSpec((TM_IN, LANES), rope_idx),
            pl.BlockSpec((TM_IN, LANES), rope_idx),
            pl.BlockSpec((TM_IN, LANES), rope_idx),
        ],
        out_specs=pl.BlockSpec((TM_IN, TN_IN), lambda i, j: (i, j)),
        out_shape=jax.ShapeDtypeStruct((rows, N_IN), BF16),
        scratch_shapes=[pltpu.VMEM((TM_IN, d), BF16)],
        compiler_params=_cparams(("arbitrary", "arbitrary")),
        name="in_projection",
    )(x, mod, mod, w_in, cos, sa, sb)


_NT = (((1,), (1,)), ((), ()))


def _lam_value(lam_ref, lam_init):
    lq = lam_ref[...]
    return (jnp.exp(jnp.sum(lq[0:1] * lq[1:2], axis=1, keepdims=True))
            - jnp.exp(jnp.sum(lq[2:3] * lq[3:4], axis=1, keepdims=True)) + lam_init)


def _split_maps(q):
    lane = lax.broadcasted_iota(jnp.int32, q.shape, 1)
    zero = jnp.zeros_like(q)
    return jnp.where(lane < QK_DIM, q, zero), jnp.where(lane >= QK_DIM, q, zero)


def _head_norm(o, g_ref, lam_init):
    y = o * lax.rsqrt(jnp.mean(o * o, axis=-1, keepdims=True) + LN_EPS)
    return (y * g_ref[...] * (1.0 - lam_init)).astype(BF16)


def _attn_latent_kernel(lam_ref, g_ref, q_ref, kl_ref, kc_ref, vl_ref, vc_ref, o_ref,
                        s_even, m_even, s_odd, m_odd, *, lam_init, seq, ctx_len):
    t = pl.program_id(0)
    lam = _lam_value(lam_ref, lam_init)
    q1, q2 = _split_maps(q_ref[...])
    tq = q_ref.shape[0]

    @pl.when(t == 0)
    def _():
        s_odd[...] = jnp.zeros_like(s_odd)
        m_odd[...] = jnp.zeros_like(m_odd)

    def step(s_w, m_w, s_r, m_r):
        for mp, qm in enumerate((q1, q2)):
            sl = lax.dot_general(qm, kl_ref[...], _NT, preferred_element_type=F32)
            sc = lax.dot_general(qm, kc_ref[...], _NT, preferred_element_type=F32)
            m = jnp.maximum(jnp.max(sl, axis=-1, keepdims=True),
                            jnp.max(sc, axis=-1, keepdims=True))
            s_w[mp, :, 0:seq] = sl
            s_w[mp, :, seq:seq + ctx_len] = sc
            m_w[mp] = jnp.broadcast_to(m, (tq, LANES))
        e1 = jnp.exp2(s_r[0] - m_r[0][:, 0:1])
        e2 = jnp.exp2(s_r[1] - m_r[1][:, 0:1])
        c1 = 1.0 / jnp.sum(e1, axis=-1, keepdims=True)
        c2 = lam / jnp.sum(e2, axis=-1, keepdims=True)
        a = (e1 * c1 - e2 * c2).astype(BF16)
        o = (jnp.dot(a[:, 0:seq], vl_ref[...], preferred_element_type=F32)
             + jnp.dot(a[:, seq:seq + ctx_len], vc_ref[...], preferred_element_type=F32))
        o_ref[...] = _head_norm(o, g_ref, lam_init)

    @pl.when(t % 2 == 0)
    def _():
        step(s_even, m_even, s_odd, m_odd)

    @pl.when(t % 2 == 1)
    def _():
        step(s_odd, m_odd, s_even, m_even)


def _attn_context_kernel(lam_ref, g_ref, q_ref, k_ref, v_ref, _, o_ref, *, lam_init):
    lam = _lam_value(lam_ref, lam_init)
    q1, q2 = _split_maps(q_ref[...])
    s1 = lax.dot_general(q1, k_ref[...], _NT, preferred_element_type=F32)
    s2 = lax.dot_general(q2, k_ref[...], _NT, preferred_element_type=F32)
    e1 = jnp.exp2(s1 - jnp.max(s1, axis=-1, keepdims=True))
    e2 = jnp.exp2(s2 - jnp.max(s2, axis=-1, keepdims=True))
    c1 = 1.0 / jnp.sum(e1, axis=-1, keepdims=True)
    c2 = lam / jnp.sum(e2, axis=-1, keepdims=True)
    a = (e1 * c1 - e2 * c2).astype(BF16)
    o = jnp.dot(a, v_ref[...], preferred_element_type=F32)
    o_ref[...] = _head_norm(o, g_ref, lam_init)


def _attention(p, lam_qk, subln_g, layer, lam_init, n_batch, seq, ctx_len):
    rows = p.shape[0]
    n_q = seq // ATT_TQ
    n_tiles = n_batch * ATT_HEADS * n_q
    ctx_blk = n_batch * seq // ctx_len
    kcol, vcol, qcol = OFF_K // LANES, OFF_V // LANES, OFF_Q // LANES
    width = ATT_HEADS * V_DIM

    def decode(tile):
        qi = tile % n_q
        bh = tile // n_q
        return bh // ATT_HEADS, bh % ATT_HEADS, qi

    def first(t):
        return decode(jnp.minimum(t, n_tiles - 1))

    def second(t):
        return decode(jnp.maximum(t - 1, 0))

    def q_idx(t):
        b, h, qi = first(t)
        return (b * n_q + qi, qcol + h)

    def o_idx(t):
        b, h, qi = second(t)
        return (b * n_q + qi, h)

    def kv_idx(which, col0, blk0):
        def idx(t):
            b, h, _ = which(t)
            return (blk0 + b, col0 + h)
        return idx

    lam_spec = pl.BlockSpec((None, 4, QK_DIM), lambda *_: (layer, 0, 0))
    g_spec = pl.BlockSpec((None, 1, V_DIM), lambda *_: (layer, 0, 0))
    s_shape = pltpu.VMEM((2, ATT_TQ, seq + ctx_len), F32)
    m_shape = pltpu.VMEM((2, ATT_TQ, LANES), F32)
    att = pl.pallas_call(
        functools.partial(_attn_latent_kernel, lam_init=lam_init, seq=seq, ctx_len=ctx_len),
        grid=(n_tiles + 1,),
        in_specs=[
            lam_spec, g_spec,
            pl.BlockSpec((ATT_TQ, LANES), q_idx),
            pl.BlockSpec((seq, LANES), kv_idx(first, kcol, 0)),
            pl.BlockSpec((ctx_len, LANES), kv_idx(first, kcol, ctx_blk)),
            pl.BlockSpec((seq, LANES), kv_idx(second, vcol, 0)),
            pl.BlockSpec((ctx_len, LANES), kv_idx(second, vcol, ctx_blk)),
        ],
        out_specs=pl.BlockSpec((ATT_TQ, LANES), o_idx),
        out_shape=jax.ShapeDtypeStruct((rows, width), BF16),
        scratch_shapes=[s_shape, m_shape, s_shape, m_shape],
        compiler_params=_cparams(("arbitrary",)),
        name="diff_attention",
    )(lam_qk, subln_g, p, p, p, p, p)

    return pl.pallas_call(
        functools.partial(_attn_context_kernel, lam_init=lam_init),
        grid=(n_batch, ATT_HEADS),
        in_specs=[
            lam_spec, g_spec,
            pl.BlockSpec((ctx_len, LANES), lambda b, h: (ctx_blk + b, qcol + h)),
            pl.BlockSpec((ctx_len, LANES), lambda b, h: (ctx_blk + b, kcol + h)),
            pl.BlockSpec((ctx_len, LANES), lambda b, h: (ctx_blk + b, vcol + h)),
            pl.BlockSpec(memory_space=pl.ANY),
        ],
        out_specs=pl.BlockSpec((ctx_len, LANES), lambda b, h: (ctx_blk + b, h)),
        out_shape=jax.ShapeDtypeStruct((rows, width), BF16),
        input_output_aliases={5: 0},
        compiler_params=_cparams(("arbitrary", "arbitrary")),
        name="diff_attention_ctx",
    )(lam_qk, subln_g, p, p, p, att)


def _fourier_kernel(*refs, length, norm, aliased):
    if aliased:
        x_ref, cs_ref, w_ref, _, o_ref, xcs_ref = refs
    else:
        x_ref, cs_ref, w_ref, o_ref, xcs_ref = refs

    @pl.when(pl.program_id(1) == 0)
    def _():
        for g in range(FOURIER_GROUPS):
            t = jnp.dot(x_ref[:, g * GROUP_C:(g + 1) * GROUP_C], cs_ref[...],
                        preferred_element_type=F32)
            xcs_ref[0:length, g * GROUP_C:(g + 1) * GROUP_C] = t[:, :GROUP_C].astype(BF16)
            xcs_ref[length:2 * length, g * GROUP_C:(g + 1) * GROUP_C] = t[:, GROUP_C:].astype(BF16)

    y = jnp.dot(w_ref[...], xcs_ref[...], preferred_element_type=F32)
    o_ref[...] = (y * norm).astype(BF16)


def _fourier(p, cs, w_pos, length, row_blk0, n_batch, prev=None):
    rows = p.shape[0]
    width = FOURIER_GROUPS * GROUP_C
    tr = min(TR_FOUR, length)
    n_r = length // tr
    norm = 1.0 / math.sqrt(length * GROUP_C)
    in_specs = [
        pl.BlockSpec((length, width), lambda b, r: (row_blk0 + b, OFF_F // width)),
        pl.BlockSpec((GROUP_C, 2 * GROUP_C), lambda b, r: (0, 0)),
        pl.BlockSpec((tr, 2 * length), lambda b, r: (r, 0)),
    ]
    args = [p, cs, w_pos]
    aliases = {}
    if prev is not None:
        in_specs.append(pl.BlockSpec(memory_space=pl.ANY))
        args.append(prev)
        aliases = {3: 0}
    kern = functools.partial(_fourier_kernel, length=length, norm=norm, aliased=prev is not None)
    return pl.pallas_call(
        kern,
        grid=(n_batch, n_r),
        in_specs=in_specs,
        out_specs=pl.BlockSpec((tr, width), lambda b, r: ((row_blk0 + b) * n_r + r, 0)),
        out_shape=jax.ShapeDtypeStruct((rows, width), BF16),
        scratch_shapes=[pltpu.VMEM((2 * length, width), BF16)],
        input_output_aliases=aliases,
        compiler_params=_cparams(("arbitrary", "arbitrary")),
        name=f"fourier_{length}",
    )(*args)


def _pool_kernel(*refs, length, aliased):
    if aliased:
        x_ref, band_ref, wg_ref, sc_ref, _, o_ref, xp_ref = refs
    else:
        x_ref, band_ref, wg_ref, sc_ref, o_ref, xp_ref = refs
    width = x_ref.shape[1]
    t_rows = min(POOL_T, length)
    slab_rows = t_rows + 2 * POOL_PAD
    xp_ref[0:POOL_PAD, :] = jnp.zeros((POOL_PAD, width), BF16)
    xp_ref[POOL_PAD + length:2 * POOL_PAD + length, :] = jnp.zeros((POOL_PAD, width), BF16)
    xp_ref[POOL_PAD:POOL_PAD + length, :] = x_ref[...]
    for i in range(length // t_rows):
        r0 = i * t_rows
        t = r0 + lax.broadcasted_iota(jnp.int32, (t_rows, 1), 0)
        for g, w in enumerate(POOL_WINDOWS):
            lo = w // 2
            hi = w - lo
            cols = slice(g * GROUP_C, (g + 1) * GROUP_C)
            slab = xp_ref[r0:r0 + slab_rows, cols]
            sums = jnp.dot(band_ref[g], slab, preferred_element_type=F32)
            cnt = (jnp.minimum(t + hi, length) - jnp.maximum(t - lo, 0)).astype(F32)
            d = sums / cnt - x_ref[r0:r0 + t_rows, cols].astype(F32)
            y = jnp.dot(d.astype(BF16), wg_ref[g], preferred_element_type=F32)
            o_ref[r0:r0 + t_rows, cols] = (y * sc_ref[:, cols]).astype(BF16)


def _pool_bands(t_rows):
    tl = np.arange(t_rows)[:, None] + POOL_PAD
    jl = np.arange(t_rows + 2 * POOL_PAD)[None, :]
    bands = []
    for w in POOL_WINDOWS:
        lo = w // 2
        hi = w - lo
        bands.append(((jl >= tl - lo) & (jl < tl + hi)).astype(np.float32))
    return jnp.asarray(np.stack(bands), dtype=BF16)


def _pool(p, w_grp, scale, layer, length, row_blk0, n_batch, prev=None):
    rows = p.shape[0]
    width = len(POOL_WINDOWS) * GROUP_C
    t_rows = min(POOL_T, length)
    bands = _pool_bands(t_rows)
    in_specs = [
        pl.BlockSpec((length, width), lambda b: (row_blk0 + b, OFF_P // width)),
        pl.BlockSpec(bands.shape, lambda b: (0, 0, 0)),
        pl.BlockSpec((None, len(POOL_WINDOWS), GROUP_C, GROUP_C), lambda b: (layer, 0, 0, 0)),
        pl.BlockSpec((None, 1, width), lambda b: (layer, 0, 0)),
    ]
    args = [p, bands, w_grp, scale]
    aliases = {}
    if prev is not None:
        in_specs.append(pl.BlockSpec(memory_space=pl.ANY))
        args.append(prev)
        aliases = {4: 0}
    kern = functools.partial(_pool_kernel, length=length, aliased=prev is not None)
    return pl.pallas_call(
        kern,
        grid=(n_batch,),
        in_specs=in_specs,
        out_specs=pl.BlockSpec((length, width), lambda b: (row_blk0 + b, 0)),
        out_shape=jax.ShapeDtypeStruct((rows, width), BF16),
        scratch_shapes=[pltpu.VMEM((length + 2 * POOL_PAD, width), BF16)],
        input_output_aliases=aliases,
        compiler_params=_cparams(("arbitrary",)),
        name=f"pool_{length}",
    )(*args)


def _merge_kernel(att_ref, four_ref, pool_ref, g0_ref, g1_ref, g2_ref, x_ref, gm_ref,
                  lng_ref, lnb_ref, wa_ref, wf_ref, wp_ref, wo_ref, o_ref, *, alpha):
    ba = jnp.dot(att_ref[...], wa_ref[...], preferred_element_type=F32)
    bf = jnp.dot(four_ref[...], wf_ref[...], preferred_element_type=F32)
    bp = jnp.dot(pool_ref[...], wp_ref[...], preferred_element_type=F32)
    m = (g0_ref[...].astype(F32) * ba + g1_ref[...].astype(F32) * bf
         + g2_ref[...].astype(F32) * bp)
    y = jnp.dot(m.astype(BF16), wo_ref[...], preferred_element_type=F32)
    z = alpha * x_ref[...] + gm_ref[...] * y
    o_ref[...] = _layer_norm(z) * lng_ref[...] + lnb_ref[...]


def _merge(att, four, pool, p, x, mod, ln_g, ln_b, w_att, w_four, w_pool, w_out, layer,
           alpha, tiles_per_batch, n_batch):
    rows, d = x.shape
    tm = TM_MIX
    gcol = OFF_G // d

    def mod_idx(i):
        return layer * MOD_ROWS + jnp.minimum(i // tiles_per_batch, n_batch)

    def const3(i):
        return (layer, 0, 0)

    return pl.pallas_call(
        functools.partial(_merge_kernel, alpha=alpha),
        grid=(rows // tm,),
        in_specs=[
            pl.BlockSpec((tm, att.shape[1]), lambda i: (i, 0)),
            pl.BlockSpec((tm, four.shape[1]), lambda i: (i, 0)),
            pl.BlockSpec((tm, pool.shape[1]), lambda i: (i, 0)),
            pl.BlockSpec((tm, d), lambda i: (i, gcol)),
            pl.BlockSpec((tm, d), lambda i: (i, gcol + 1)),
            pl.BlockSpec((tm, d), lambda i: (i, gcol + 2)),
            pl.BlockSpec((tm, d), lambda i: (i, 0)),
            pl.BlockSpec((None, 1, d), lambda i: (mod_idx(i), 0, 2)),
            pl.BlockSpec((None, 1, d), const3),
            pl.BlockSpec((None, 1, d), const3),
            pl.BlockSpec((None,) + w_att.shape[1:], const3),
            pl.BlockSpec((None,) + w_four.shape[1:], const3),
            pl.BlockSpec((None,) + w_pool.shape[1:], const3),
            pl.BlockSpec((None,) + w_out.shape[1:], const3),
        ],
        out_specs=pl.BlockSpec((tm, d), lambda i: (i, 0)),
        out_shape=jax.ShapeDtypeStruct((rows, d), F32),
        compiler_params=_cparams(("arbitrary",)),
        name="branch_merge",
    )(att, four, pool, p, p, p, x, mod, ln_g, ln_b, w_att, w_four, w_pool, w_out)


def _ffn_kernel(x_ref, sh_ref, sc_ref, gm_ref, lng_ref, lnb_ref, wg_ref, wu_ref, wd_ref,
                o_ref, h_ref, acc_ref, *, alpha):
    j = pl.program_id(1)

    @pl.when(j == 0)
    def _():
        h = _layer_norm(x_ref[...])
        h_ref[...] = (h * (1.0 + sc_ref[...]) + sh_ref[...]).astype(BF16)
        acc_ref[...] = jnp.zeros_like(acc_ref)

    h = h_ref[...]
    gate = jnp.dot(h, wg_ref[...], preferred_element_type=F32)
    up = jnp.dot(h, wu_ref[...], preferred_element_type=F32)
    a = (gate * _sigmoid(gate) * up).astype(BF16)
    acc_ref[...] += jnp.dot(a, wd_ref[...], preferred_element_type=F32)

    @pl.when(j == pl.num_programs(1) - 1)
    def _():
        z = alpha * x_ref[...] + gm_ref[...] * acc_ref[...]
        o_ref[...] = _layer_norm(z) * lng_ref[...] + lnb_ref[...]


def _ffn(x, mod, ln_g, ln_b, w_gate, w_up, w_down, layer, alpha, tiles_per_batch, n_batch):
    rows, d = x.shape
    d_ff = w_gate.shape[2]
    tm = TM_FFN
    tf = d_ff // 2

    def mod_idx(i):
        return layer * MOD_ROWS + jnp.minimum(i // tiles_per_batch, n_batch)

    return pl.pallas_call(
        functools.partial(_ffn_kernel, alpha=alpha),
        grid=(rows // tm, d_ff // tf),
        in_specs=[
            pl.BlockSpec((tm, d), lambda i, j: (i, 0)),
            pl.BlockSpec((None, 1, d), lambda i, j: (mod_idx(i), 0, 3)),
            pl.BlockSpec((None, 1, d), lambda i, j: (mod_idx(i), 0, 4)),
            pl.BlockSpec((None, 1, d), lambda i, j: (mod_idx(i), 0, 5)),
            pl.BlockSpec((None, 1, d), lambda i, j: (layer, 0, 0)),
            pl.BlockSpec((None, 1, d), lambda i, j: (layer, 0, 0)),
            pl.BlockSpec((None, d, tf), lambda i, j: (layer, 0, j)),
            pl.BlockSpec((None, d, tf), lambda i, j: (layer, 0, j)),
            pl.BlockSpec((None, tf, d), lambda i, j: (layer, j, 0)),
        ],
        out_specs=pl.BlockSpec((tm, d), lambda i, j: (i, 0)),
        out_shape=jax.ShapeDtypeStruct((rows, d), F32),
        scratch_shapes=[pltpu.VMEM((tm, d), BF16), pltpu.VMEM((tm, d), F32)],
        compiler_params=_cparams(("arbitrary", "arbitrary")),
        name="swiglu",
    )(x, mod, mod, mod, ln_g, ln_b, w_gate, w_up, w_down)


def _rope_tables(seq):
    rows = seq // GRID_W
    row = np.repeat(np.arange(rows), GRID_W).astype(np.float32)
    col = np.tile(np.arange(GRID_W), rows).astype(np.float32)
    inv = (ROPE_BASE ** (-np.arange(ROPE_FREQS, dtype=np.float32) / ROPE_FREQS)).astype(np.float32)
    ar = row[:, None] * inv[None, :]
    ac = col[:, None] * inv[None, :]
    ang = np.concatenate([ar, ar, ac, ac], axis=-1).astype(np.float32)
    cos = np.cos(ang).astype(np.float32)
    sin = np.sin(ang).astype(np.float32)
    first = (np.arange(QK_DIM) % (2 * ROPE_FREQS)) < ROPE_FREQS
    sa = np.where(first[None, :], -sin, 0.0)
    sb = np.where(first[None, :], 0.0, sin)
    reps = LANES // QK_DIM

    def widen(t, fill):
        t = np.tile(t, (1, reps))
        ident = np.full((TM_IN, LANES), fill, np.float32)
        return jnp.asarray(np.concatenate([t, ident], axis=0), dtype=F32)

    return widen(cos, 1.0), widen(sa, 0.0), widen(sb, 0.0)


def _dft_tables(length):
    def cos_sin(n):
        k = np.arange(n, dtype=np.int64)
        ang = 2.0 * np.pi * ((k[:, None] * k[None, :]) % n).astype(np.float64) / n
        return np.cos(ang), np.sin(ang)

    cc, sc = cos_sin(GROUP_C)
    cl, sl = cos_sin(length)
    cs = jnp.asarray(np.concatenate([cc, sc], axis=1), dtype=F32).astype(BF16)
    w_pos = jnp.asarray(np.concatenate([cl, -sl], axis=1), dtype=F32).astype(BF16)
    return cs, w_pos


def kernel(x, c, ctx, c_ctx, w_mod, b_mod, w_in, lam_qk, subln_g, w_att_br, w_four_br,
           w_pool_grp, pool_scale, w_pool_br, w_out, ln1_g, ln1_b, w_ffn_gate, w_ffn_up,
           w_ffn_down, ln2_g, ln2_b):
    n_batch, seq, d = x.shape
    ctx_len = ctx.shape[1]
    depth = w_mod.shape[0]
    assert d == D_MODEL and seq % ATT_TQ == 0 and seq % TM_IN == 0 and n_batch < MOD_ROWS
    assert (n_batch * ctx_len) % TM_IN == 0 and seq % ctx_len == 0
    alpha = (2 * depth) ** 0.25

    xs = jnp.concatenate([x.reshape(n_batch * seq, d), ctx.reshape(n_batch * ctx_len, d)], axis=0)

    cond = jnp.zeros((MOD_ROWS, d), F32).at[:n_batch].set(c).at[n_batch].set(c_ctx)
    mod = _mod_vectors(cond, w_mod, b_mod).reshape(depth * MOD_ROWS, 1, 6 * d)

    bf = lambda w: w.astype(BF16)
    w_in_b, w_att_b, w_four_b, w_pool_b, w_out_b = map(bf, (w_in, w_att_br, w_four_br, w_pool_br, w_out))
    w_grp_b, w_g_b, w_u_b, w_d_b = map(bf, (w_pool_grp, w_ffn_gate, w_ffn_up, w_ffn_down))
    vec3 = lambda v: v.reshape(depth, 1, v.shape[-1])
    subln3, pscale3 = vec3(subln_g), vec3(pool_scale)
    ln1_g3, ln1_b3, ln2_g3, ln2_b3 = map(vec3, (ln1_g, ln1_b, ln2_g, ln2_b))

    rope = _rope_tables(seq)
    cs_lat, wpos_lat = _dft_tables(seq)
    cs_ctx, wpos_ctx = _dft_tables(ctx_len)

    n_lat_rows = n_batch * seq
    ctx_blk0 = n_lat_rows // ctx_len

    for l in range(depth):
        lam_init = 0.8 - 0.6 * math.exp(-0.3 * l)
        p = _in_projection(xs, mod, w_in_b, rope, l, n_lat_rows // TM_IN, seq // TM_IN, n_batch)
        att = _attention(p, lam_qk, subln3, l, lam_init, n_batch, seq, ctx_len)
        four = _fourier(p, cs_lat, wpos_lat, seq, 0, n_batch)
        four = _fourier(p, cs_ctx, wpos_ctx, ctx_len, ctx_blk0, n_batch, prev=four)
        pool = _pool(p, w_grp_b, pscale3, l, seq, 0, n_batch)
        pool = _pool(p, w_grp_b, pscale3, l, ctx_len, ctx_blk0, n_batch, prev=pool)
        xs = _merge(att, four, pool, p, xs, mod, ln1_g3, ln1_b3, w_att_b, w_four_b, w_pool_b,
                    w_out_b, l, alpha, seq // TM_MIX, n_batch)
        xs = _ffn(xs, mod, ln2_g3, ln2_b3, w_g_b, w_u_b, w_d_b, l, alpha, seq // TM_FFN, n_batch)

    return xs[:n_lat_rows].reshape(n_batch, seq, d)
```

```python
import functools
import math

import jax
import jax.numpy as jnp
import numpy as np
from jax import lax
from jax.experimental import pallas as pl
from jax.experimental.pallas import tpu as pltpu

F32 = jnp.float32
BF16 = jnp.bfloat16

D_MODEL = 1024
GRID_W = 64
ATT_HEADS = 8
QK_DIM = 64
V_DIM = 128
ROPE_FREQS = 16
ROPE_BASE = 10000.0
FOURIER_GROUPS = 4
GROUP_C = 128
POOL_WINDOWS = (2, 4, 8, 16)
OFF_K, OFF_V, OFF_Q, OFF_F, OFF_P, OFF_G = 0, 1024, 2048, 3072, 3584, 4096
N_IN = 7168
LN_EPS = 1e-5
LOG2_E = math.log2(math.e)

LANES = 128
VMEM_LIMIT = 56 * 1024 * 1024

TM_IN = 1024
TN_IN = 1024
ATT_TQ = 512
TM_MIX = 512
TM_FFN = 512
TR_FOUR = 512
POOL_T = 256
POOL_PAD = 128
MOD_ROWS = 16


def _cparams(sem):
    return pltpu.CompilerParams(dimension_semantics=sem, vmem_limit_bytes=VMEM_LIMIT)


def _layer_norm(x):
    mu = jnp.mean(x, axis=-1, keepdims=True)
    xc = x - mu
    var = jnp.mean(xc * xc, axis=-1, keepdims=True)
    return xc * lax.rsqrt(var + LN_EPS)


def _sigmoid(x):
    return 1.0 / (1.0 + jnp.exp(-x))


def _mod_kernel(c_ref, w_ref, b_ref, o_ref):
    c = c_ref[...]
    sc = (c * _sigmoid(c)).astype(BF16)
    o_ref[...] = jnp.dot(sc, w_ref[...].astype(BF16), preferred_element_type=F32) + b_ref[...]


def _mod_vectors(cond, w_mod, b_mod):
    depth, d, n = w_mod.shape
    tn = 1024
    return pl.pallas_call(
        _mod_kernel,
        grid=(depth, n // tn),
        in_specs=[
            pl.BlockSpec((MOD_ROWS, d), lambda l, j: (0, 0)),
            pl.BlockSpec((None, d, tn), lambda l, j: (l, 0, j)),
            pl.BlockSpec((None, 1, tn), lambda l, j: (l, 0, j)),
        ],
        out_specs=pl.BlockSpec((None, MOD_ROWS, tn), lambda l, j: (l, 0, j)),
        out_shape=jax.ShapeDtypeStruct((depth, MOD_ROWS, n), F32),
        compiler_params=_cparams(("arbitrary", "arbitrary")),
        name="mod_vectors",
    )(cond, w_mod, b_mod.reshape(depth, 1, n))


def _inproj_kernel(x_ref, sh_ref, sc_ref, w_ref, cos_ref, sa_ref, sb_ref, o_ref, u_ref):
    j = pl.program_id(1)

    @pl.when(j == 0)
    def _():
        h = _layer_norm(x_ref[...])
        u_ref[...] = (h * (1.0 + sc_ref[...]) + sh_ref[...]).astype(BF16)

    acc = jnp.dot(u_ref[...], w_ref[...], preferred_element_type=F32)
    is_k = j == OFF_K // TN_IN
    is_q = j == OFF_Q // TN_IN
    is_gate = j >= OFF_G // TN_IN

    @pl.when(is_k | is_q)
    def _():
        scale = jnp.where(is_q, QK_DIM ** -0.5 * LOG2_E, 1.0).astype(F32)
        cos, sa, sb = cos_ref[...], sa_ref[...], sb_ref[...]
        for c in range(TN_IN // LANES):
            xs = acc[:, c * LANES:(c + 1) * LANES]
            r = (xs * cos + pltpu.roll(xs, LANES - ROPE_FREQS, 1) * sa
                 + pltpu.roll(xs, ROPE_FREQS, 1) * sb)
            o_ref[:, c * LANES:(c + 1) * LANES] = (r * scale).astype(BF16)

    @pl.when(is_gate)
    def _():
        o_ref[...] = _sigmoid(acc).astype(BF16)

    @pl.when(jnp.logical_not(is_k | is_q | is_gate))
    def _():
        o_ref[...] = acc.astype(BF16)


def _in_projection(x, mod, w_in, rope, layer, n_lat_tiles, tiles_per_batch, n_batch):
    rows, d = x.shape
    cos, sa, sb = rope
    n_rope_blocks = cos.shape[0] // TM_IN - 1

    def mod_idx(i):
        return layer * MOD_ROWS + jnp.minimum(i // tiles_per_batch, n_batch)

    def rope_idx(i, j):
        return (jnp.where(i < n_lat_tiles, i % tiles_per_batch, n_rope_blocks), 0)

    return pl.pallas_call(
        _inproj_kernel,
        grid=(rows // TM_IN, N_IN // TN_IN),
        in_specs=[
            pl.BlockSpec((TM_IN, d), lambda i, j: (i, 0)),
            pl.BlockSpec((None, 1, d), lambda i, j: (mod_idx(i), 0, 0)),
            pl.BlockSpec((None, 1, d), lambda i, j: (mod_idx(i), 0, 1)),
            pl.BlockSpec((None, d, TN_IN), lambda i, j: (layer, 0, j)),
            pl.BlockSpec((TM_IN, LANES), rope_idx),
            pl.BlockSpec((TM_IN, LANES), rope_idx),
            pl.BlockSpec((TM_IN, LANES), rope_idx),
        ],
        out_specs=pl.BlockSpec((TM_IN, TN_IN), lambda i, j: (i, j)),
        out_shape=jax.ShapeDtypeStruct((rows, N_IN), BF16),
        scratch_shapes=[pltpu.VMEM((TM_IN, d), BF16)],
        compiler_params=_cparams(("arbitrary", "arbitrary")),
        name="in_projection",
    )(x, mod, mod, w_in, cos, sa, sb)


_NT = (((1,), (1,)), ((), ()))


def _lam_value(lam_ref, lam_init):
    lq = lam_ref[...]
    return (jnp.exp(jnp.sum(lq[0:1] * lq[1:2], axis=1, keepdims=True))
            - jnp.exp(jnp.sum(lq[2:3] * lq[3:4], axis=1, keepdims=True)) + lam_init)


def _split_maps(q):
    lane = lax.broadcasted_iota(jnp.int32, q.shape, 1)
    zero = jnp.zeros_like(q)
    return jnp.where(lane < QK_DIM, q, zero), jnp.where(lane >= QK_DIM, q, zero)


def _head_norm(o, g_ref, lam_init):
    y = o * lax.rsqrt(jnp.mean(o * o, axis=-1, keepdims=True) + LN_EPS)
    return (y * g_ref[...] * (1.0 - lam_init)).astype(BF16)


def _attn_latent_kernel(lam_ref, g_ref, q_ref, kl_ref, kc_ref, vl_ref, vc_ref, o_ref,
                        s_even, m_even, s_odd, m_odd, vx_ref, *, lam_init, seq, ctx_len, n_q):
    t = pl.program_id(0)
    lam = _lam_value(lam_ref, lam_init)
    q1, q2 = _split_maps(q_ref[...])
    tq = q_ref.shape[0]

    @pl.when(t == 0)
    def _():
        s_odd[...] = jnp.zeros_like(s_odd)
        m_odd[...] = jnp.zeros_like(m_odd)
        vx_ref[:, V_DIM:2 * V_DIM] = jnp.ones((seq + ctx_len, V_DIM), BF16)

    @pl.when(jnp.maximum(t - 1, 0) % n_q == 0)
    def _():
        vx_ref[0:seq, 0:V_DIM] = vl_ref[...]
        vx_ref[seq:seq + ctx_len, 0:V_DIM] = vc_ref[...]

    def step(s_w, m_w, s_r, m_r):
        for mp, qm in enumerate((q1, q2)):
            sl = lax.dot_general(qm, kl_ref[...], _NT, preferred_element_type=F32)
            sc = lax.dot_general(qm, kc_ref[...], _NT, preferred_element_type=F32)
            m = jnp.maximum(jnp.max(sl, axis=-1, keepdims=True),
                            jnp.max(sc, axis=-1, keepdims=True))
            s_w[mp, :, 0:seq] = sl
            s_w[mp, :, seq:seq + ctx_len] = sc
            m_w[mp] = jnp.broadcast_to(m, (tq, LANES))
        e1 = jnp.exp2(s_r[0] - m_r[0][:, 0:1]).astype(BF16)
        e2 = jnp.exp2(s_r[1] - m_r[1][:, 0:1]).astype(BF16)
        r1 = jnp.dot(e1, vx_ref[...], preferred_element_type=F32)
        r2 = jnp.dot(e2, vx_ref[...], preferred_element_type=F32)
        o = (r1[:, 0:V_DIM] / r1[:, V_DIM:2 * V_DIM]
             - lam * (r2[:, 0:V_DIM] / r2[:, V_DIM:2 * V_DIM]))
        o_ref[...] = _head_norm(o, g_ref, lam_init)

    @pl.when(t % 2 == 0)
    def _():
        step(s_even, m_even, s_odd, m_odd)

    @pl.when(t % 2 == 1)
    def _():
        step(s_odd, m_odd, s_even, m_even)


def _attn_context_kernel(lam_ref, g_ref, q_ref, k_ref, v_ref, _, o_ref, *, lam_init):
    lam = _lam_value(lam_ref, lam_init)
    q1, q2 = _split_maps(q_ref[...])
    s1 = lax.dot_general(q1, k_ref[...], _NT, preferred_element_type=F32)
    s2 = lax.dot_general(q2, k_ref[...], _NT, preferred_element_type=F32)
    e1 = jnp.exp2(s1 - jnp.max(s1, axis=-1, keepdims=True))
    e2 = jnp.exp2(s2 - jnp.max(s2, axis=-1, keepdims=True))
    c1 = 1.0 / jnp.sum(e1, axis=-1, keepdims=True)
    c2 = lam / jnp.sum(e2, axis=-1, keepdims=True)
    a = (e1 * c1 - e2 * c2).astype(BF16)
    o = jnp.dot(a, v_ref[...], preferred_element_type=F32)
    o_ref[...] = _head_norm(o, g_ref, lam_init)


def _attention(p, lam_qk, subln_g, layer, lam_init, n_batch, seq, ctx_len):
    rows = p.shape[0]
    n_q = seq // ATT_TQ
    n_tiles = n_batch * ATT_HEADS * n_q
    ctx_blk = n_batch * seq // ctx_len
    kcol, vcol, qcol = OFF_K // LANES, OFF_V // LANES, OFF_Q // LANES
    width = ATT_HEADS * V_DIM

    def decode(tile):
        qi = tile % n_q
        bh = tile // n_q
        return bh // ATT_HEADS, bh % ATT_HEADS, qi

    def first(t):
        return decode(jnp.minimum(t, n_tiles - 1))

    def second(t):
        return decode(jnp.maximum(t - 1, 0))

    def q_idx(t):
        b, h, qi = first(t)
        return (b * n_q + qi, qcol + h)

    def o_idx(t):
        b, h, qi = second(t)
        return (b * n_q + qi, h)

    def kv_idx(which, col0, blk0):
        def idx(t):
            b, h, _ = which(t)
            return (blk0 + b, col0 + h)
        return idx

    lam_spec = pl.BlockSpec((None, 4, QK_DIM), lambda *_: (layer, 0, 0))
    g_spec = pl.BlockSpec((None, 1, V_DIM), lambda *_: (layer, 0, 0))
    s_shape = pltpu.VMEM((2, ATT_TQ, seq + ctx_len), F32)
    m_shape = pltpu.VMEM((2, ATT_TQ, LANES), F32)
    att = pl.pallas_call(
        functools.partial(_attn_latent_kernel, lam_init=lam_init, seq=seq, ctx_len=ctx_len,
                          n_q=n_q),
        grid=(n_tiles + 1,),
        in_specs=[
            lam_spec, g_spec,
            pl.BlockSpec((ATT_TQ, LANES), q_idx),
            pl.BlockSpec((seq, LANES), kv_idx(first, kcol, 0)),
            pl.BlockSpec((ctx_len, LANES), kv_idx(first, kcol, ctx_blk)),
            pl.BlockSpec((seq, LANES), kv_idx(second, vcol, 0)),
            pl.BlockSpec((ctx_len, LANES), kv_idx(second, vcol, ctx_blk)),
        ],
        out_specs=pl.BlockSpec((ATT_TQ, LANES), o_idx),
        out_shape=jax.ShapeDtypeStruct((rows, width), BF16),
        scratch_shapes=[s_shape, m_shape, s_shape, m_shape,
                        pltpu.VMEM((seq + ctx_len, 2 * V_DIM), BF16)],
        compiler_params=_cparams(("arbitrary",)),
        name="diff_attention",
    )(lam_qk, subln_g, p, p, p, p, p)

    return pl.pallas_call(
        functools.partial(_attn_context_kernel, lam_init=lam_init),
        grid=(n_batch, ATT_HEADS),
        in_specs=[
            lam_spec, g_spec,
            pl.BlockSpec((ctx_len, LANES), lambda b, h: (ctx_blk + b, qcol + h)),
            pl.BlockSpec((ctx_len, LANES), lambda b, h: (ctx_blk + b, kcol + h)),
            pl.BlockSpec((ctx_len, LANES), lambda b, h: (ctx_blk + b, vcol + h)),
            pl.BlockSpec(memory_space=pl.ANY),
        ],
        out_specs=pl.BlockSpec((ctx_len, LANES), lambda b, h: (ctx_blk + b, h)),
        out_shape=jax.ShapeDtypeStruct((rows, width), BF16),
        input_output_aliases={5: 0},
        compiler_params=_cparams(("arbitrary", "arbitrary")),
        name="diff_attention_ctx",
    )(lam_qk, subln_g, p, p, p, att)


def _fourier_kernel(*refs, length, norm, aliased):
    if aliased:
        x_ref, cs_ref, w_ref, _, o_ref, xcs_ref = refs
    else:
        x_ref, cs_ref, w_ref, o_ref, xcs_ref = refs

    @pl.when(pl.program_id(1) == 0)
    def _():
        for g in range(FOURIER_GROUPS):
            t = jnp.dot(x_ref[:, g * GROUP_C:(g + 1) * GROUP_C], cs_ref[...],
                        preferred_element_type=F32)
            xcs_ref[0:length, g * GROUP_C:(g + 1) * GROUP_C] = t[:, :GROUP_C].astype(BF16)
            xcs_ref[length:2 * length, g * GROUP_C:(g + 1) * GROUP_C] = t[:, GROUP_C:].astype(BF16)

    y = jnp.dot(w_ref[...], xcs_ref[...], preferred_element_type=F32)
    o_ref[...] = (y * norm).astype(BF16)


def _fourier(p, cs, w_pos, length, row_blk0, n_batch, prev=None):
    rows = p.shape[0]
    width = FOURIER_GROUPS * GROUP_C
    tr = min(TR_FOUR, length)
    n_r = length // tr
    norm = 1.0 / math.sqrt(length * GROUP_C)
    in_specs = [
        pl.BlockSpec((length, width), lambda b, r: (row_blk0 + b, OFF_F // width)),
        pl.BlockSpec((GROUP_C, 2 * GROUP_C), lambda b, r: (0, 0)),
        pl.BlockSpec((tr, 2 * length), lambda b, r: (r, 0)),
    ]
    args = [p, cs, w_pos]
    aliases = {}
    if prev is not None:
        in_specs.append(pl.BlockSpec(memory_space=pl.ANY))
        args.append(prev)
        aliases = {3: 0}
    kern = functools.partial(_fourier_kernel, length=length, norm=norm, aliased=prev is not None)
    return pl.pallas_call(
        kern,
        grid=(n_batch, n_r),
        in_specs=in_specs,
        out_specs=pl.BlockSpec((tr, width), lambda b, r: ((row_blk0 + b) * n_r + r, 0)),
        out_shape=jax.ShapeDtypeStruct((rows, width), BF16),
        scratch_shapes=[pltpu.VMEM((2 * length, width), BF16)],
        input_output_aliases=aliases,
        compiler_params=_cparams(("arbitrary", "arbitrary")),
        name=f"fourier_{length}",
    )(*args)


def _pool_kernel(*refs, length, aliased):
    if aliased:
        x_ref, band_ref, wg_ref, sc_ref, _, o_ref, xp_ref = refs
    else:
        x_ref, band_ref, wg_ref, sc_ref, o_ref, xp_ref = refs
    width = x_ref.shape[1]
    t_rows = min(POOL_T, length)
    slab_rows = t_rows + 2 * POOL_PAD
    xp_ref[0:POOL_PAD, :] = jnp.zeros((POOL_PAD, width), BF16)
    xp_ref[POOL_PAD + length:2 * POOL_PAD + length, :] = jnp.zeros((POOL_PAD, width), BF16)
    xp_ref[POOL_PAD:POOL_PAD + length, :] = x_ref[...]
    for i in range(length // t_rows):
        r0 = i * t_rows
        t = r0 + lax.broadcasted_iota(jnp.int32, (t_rows, 1), 0)
        for g, w in enumerate(POOL_WINDOWS):
            lo = w // 2
            hi = w - lo
            cols = slice(g * GROUP_C, (g + 1) * GROUP_C)
            slab = xp_ref[r0:r0 + slab_rows, cols]
            sums = jnp.dot(band_ref[g], slab, preferred_element_type=F32)
            cnt = (jnp.minimum(t + hi, length) - jnp.maximum(t - lo, 0)).astype(F32)
            d = sums / cnt - x_ref[r0:r0 + t_rows, cols].astype(F32)
            y = jnp.dot(d.astype(BF16), wg_ref[g], preferred_element_type=F32)
            o_ref[r0:r0 + t_rows, cols] = (y * sc_ref[:, cols]).astype(BF16)


def _pool_bands(t_rows):
    tl = np.arange(t_rows)[:, None] + POOL_PAD
    jl = np.arange(t_rows + 2 * POOL_PAD)[None, :]
    bands = []
    for w in POOL_WINDOWS:
        lo = w // 2
        hi = w - lo
        bands.append(((jl >= tl - lo) & (jl < tl + hi)).astype(np.float32))
    return jnp.asarray(np.stack(bands), dtype=BF16)


def _pool(p, w_grp, scale, layer, length, row_blk0, n_batch, prev=None):
    rows = p.shape[0]
    width = len(POOL_WINDOWS) * GROUP_C
    t_rows = min(POOL_T, length)
    bands = _pool_bands(t_rows)
    in_specs = [
        pl.BlockSpec((length, width), lambda b: (row_blk0 + b, OFF_P // width)),
        pl.BlockSpec(bands.shape, lambda b: (0, 0, 0)),
        pl.BlockSpec((None, len(POOL_WINDOWS), GROUP_C, GROUP_C), lambda b: (layer, 0, 0, 0)),
        pl.BlockSpec((None, 1, width), lambda b: (layer, 0, 0)),
    ]
    args = [p, bands, w_grp, scale]
    aliases = {}
    if prev is not None:
        in_specs.append(pl.BlockSpec(memory_space=pl.ANY))
        args.append(prev)
        aliases = {4: 0}
    kern = functools.partial(_pool_kernel, length=length, aliased=prev is not None)
    return pl.pallas_call(
        kern,
        grid=(n_batch,),
        in_specs=in_specs,
        out_specs=pl.BlockSpec((length, width), lambda b: (row_blk0 + b, 0)),
        out_shape=jax.ShapeDtypeStruct((rows, width), BF16),
        scratch_shapes=[pltpu.VMEM((length + 2 * POOL_PAD, width), BF16)],
        input_output_aliases=aliases,
        compiler_params=_cparams(("arbitrary",)),
        name=f"pool_{length}",
    )(*args)


def _merge_kernel(att_ref, four_ref, pool_ref, g0_ref, g1_ref, g2_ref, x_ref, gm_ref,
                  lng_ref, lnb_ref, wa_ref, wf_ref, wp_ref, wo_ref, o_ref, *, alpha):
    ba = jnp.dot(att_ref[...], wa_ref[...], preferred_element_type=F32)
    bf = jnp.dot(four_ref[...], wf_ref[...], preferred_element_type=F32)
    bp = jnp.dot(pool_ref[...], wp_ref[...], preferred_element_type=F32)
    m = (g0_ref[...].astype(F32) * ba + g1_ref[...].astype(F32) * bf
         + g2_ref[...].astype(F32) * bp)
    y = jnp.dot(m.astype(BF16), wo_ref[...], preferred_element_type=F32)
    z = alpha * x_ref[...] + gm_ref[...] * y
    o_ref[...] = _layer_norm(z) * lng_ref[...] + lnb_ref[...]


def _merge(att, four, pool, p, x, mod, ln_g, ln_b, w_att, w_four, w_pool, w_out, layer,
           alpha, tiles_per_batch, n_batch):
    rows, d = x.shape
    tm = TM_MIX
    gcol = OFF_G // d

    def mod_idx(i):
        return layer * MOD_ROWS + jnp.minimum(i // tiles_per_batch, n_batch)

    def const3(i):
        return (layer, 0, 0)

    return pl.pallas_call(
        functools.partial(_merge_kernel, alpha=alpha),
        grid=(rows // tm,),
        in_specs=[
            pl.BlockSpec((tm, att.shape[1]), lambda i: (i, 0)),
            pl.BlockSpec((tm, four.shape[1]), lambda i: (i, 0)),
            pl.BlockSpec((tm, pool.shape[1]), lambda i: (i, 0)),
            pl.BlockSpec((tm, d), lambda i: (i, gcol)),
            pl.BlockSpec((tm, d), lambda i: (i, gcol + 1)),
            pl.BlockSpec((tm, d), lambda i: (i, gcol + 2)),
            pl.BlockSpec((tm, d), lambda i: (i, 0)),
            pl.BlockSpec((None, 1, d), lambda i: (mod_idx(i), 0, 2)),
            pl.BlockSpec((None, 1, d), const3),
            pl.BlockSpec((None, 1, d), const3),
            pl.BlockSpec((None,) + w_att.shape[1:], const3),
            pl.BlockSpec((None,) + w_four.shape[1:], const3),
            pl.BlockSpec((None,) + w_pool.shape[1:], const3),
            pl.BlockSpec((None,) + w_out.shape[1:], const3),
        ],
        out_specs=pl.BlockSpec((tm, d), lambda i: (i, 0)),
        out_shape=jax.ShapeDtypeStruct((rows, d), F32),
        compiler_params=_cparams(("arbitrary",)),
        name="branch_merge",
    )(att, four, pool, p, p, p, x, mod, ln_g, ln_b, w_att, w_four, w_pool, w_out)


def _ffn_kernel(x_ref, sh_ref, sc_ref, gm_ref, lng_ref, lnb_ref, wg_ref, wu_ref, wd_ref,
                o_ref, h_ref, acc_ref, *, alpha):
    j = pl.program_id(1)

    @pl.when(j == 0)
    def _():
        h = _layer_norm(x_ref[...])
        h_ref[...] = (h * (1.0 + sc_ref[...]) + sh_ref[...]).astype(BF16)
        acc_ref[...] = jnp.zeros_like(acc_ref)

    h = h_ref[...]
    gate = jnp.dot(h, wg_ref[...], preferred_element_type=F32)
    up = jnp.dot(h, wu_ref[...], preferred_element_type=F32)
    a = (gate * _sigmoid(gate) * up).astype(BF16)
    acc_ref[...] += jnp.dot(a, wd_ref[...], preferred_element_type=F32)

    @pl.when(j == pl.num_programs(1) - 1)
    def _():
        z = alpha * x_ref[...] + gm_ref[...] * acc_ref[...]
        o_ref[...] = _layer_norm(z) * lng_ref[...] + lnb_ref[...]


def _ffn(x, mod, ln_g, ln_b, w_gate, w_up, w_down, layer, alpha, tiles_per_batch, n_batch):
    rows, d = x.shape
    d_ff = w_gate.shape[2]
    tm = TM_FFN
    tf = d_ff // 2

    def mod_idx(i):
        return layer * MOD_ROWS + jnp.minimum(i // tiles_per_batch, n_batch)

    return pl.pallas_call(
        functools.partial(_ffn_kernel, alpha=alpha),
        grid=(rows // tm, d_ff // tf),
        in_specs=[
            pl.BlockSpec((tm, d), lambda i, j: (i, 0)),
            pl.BlockSpec((None, 1, d), lambda i, j: (mod_idx(i), 0, 3)),
            pl.BlockSpec((None, 1, d), lambda i, j: (mod_idx(i), 0, 4)),
            pl.BlockSpec((None, 1, d), lambda i, j: (mod_idx(i), 0, 5)),
            pl.BlockSpec((None, 1, d), lambda i, j: (layer, 0, 0)),
            pl.BlockSpec((None, 1, d), lambda i, j: (layer, 0, 0)),
            pl.BlockSpec((None, d, tf), lambda i, j: (layer, 0, j)),
            pl.BlockSpec((None, d, tf), lambda i, j: (layer, 0, j)),
            pl.BlockSpec((None, tf, d), lambda i, j: (layer, j, 0)),
        ],
        out_specs=pl.BlockSpec((tm, d), lambda i, j: (i, 0)),
        out_shape=jax.ShapeDtypeStruct((rows, d), F32),
        scratch_shapes=[pltpu.VMEM((tm, d), BF16), pltpu.VMEM((tm, d), F32)],
        compiler_params=_cparams(("arbitrary", "arbitrary")),
        name="swiglu",
    )(x, mod, mod, mod, ln_g, ln_b, w_gate, w_up, w_down)


def _rope_tables(seq):
    rows = seq // GRID_W
    row = np.repeat(np.arange(rows), GRID_W).astype(np.float32)
    col = np.tile(np.arange(GRID_W), rows).astype(np.float32)
    inv = (ROPE_BASE ** (-np.arange(ROPE_FREQS, dtype=np.float32) / ROPE_FREQS)).astype(np.float32)
    ar = row[:, None] * inv[None, :]
    ac = col[:, None] * inv[None, :]
    ang = np.concatenate([ar, ar, ac, ac], axis=-1).astype(np.float32)
    cos = np.cos(ang).astype(np.float32)
    sin = np.sin(ang).astype(np.float32)
    first = (np.arange(QK_DIM) % (2 * ROPE_FREQS)) < ROPE_FREQS
    sa = np.where(first[None, :], -sin, 0.0)
    sb = np.where(first[None, :], 0.0, sin)
    reps = LANES // QK_DIM

    def widen(t, fill):
        t = np.tile(t, (1, reps))
        ident = np.full((TM_IN, LANES), fill, np.float32)
        return jnp.asarray(np.concatenate([t, ident], axis=0), dtype=F32)

    return widen(cos, 1.0), widen(sa, 0.0), widen(sb, 0.0)


def _dft_tables(length):
    def cos_sin(n):
        k = np.arange(n, dtype=np.int64)
        ang = 2.0 * np.pi * ((k[:, None] * k[None, :]) % n).astype(np.float64) / n
        return np.cos(ang), np.sin(ang)

    cc, sc = cos_sin(GROUP_C)
    cl, sl = cos_sin(length)
    cs = jnp.asarray(np.concatenate([cc, sc], axis=1), dtype=F32).astype(BF16)
    w_pos = jnp.asarray(np.concatenate([cl, -sl], axis=1), dtype=F32).astype(BF16)
    return cs, w_pos


def kernel(x, c, ctx, c_ctx, w_mod, b_mod, w_in, lam_qk, subln_g, w_att_br, w_four_br,
           w_pool_grp, pool_scale, w_pool_br, w_out, ln1_g, ln1_b, w_ffn_gate, w_ffn_up,
           w_ffn_down, ln2_g, ln2_b):
    n_batch, seq, d = x.shape
    ctx_len = ctx.shape[1]
    depth = w_mod.shape[0]
    assert d == D_MODEL and seq % ATT_TQ == 0 and seq % TM_IN == 0 and n_batch < MOD_ROWS
    assert (n_batch * ctx_len) % TM_IN == 0 and seq % ctx_len == 0
    alpha = (2 * depth) ** 0.25

    xs = jnp.concatenate([x.reshape(n_batch * seq, d), ctx.reshape(n_batch * ctx_len, d)], axis=0)

    cond = jnp.zeros((MOD_ROWS, d), F32).at[:n_batch].set(c).at[n_batch].set(c_ctx)
    mod = _mod_vectors(cond, w_mod, b_mod).reshape(depth * MOD_ROWS, 1, 6 * d)

    bf = lambda w: w.astype(BF16)
    w_in_b, w_att_b, w_four_b, w_pool_b, w_out_b = map(bf, (w_in, w_att_br, w_four_br, w_pool_br, w_out))
    w_grp_b, w_g_b, w_u_b, w_d_b = map(bf, (w_pool_grp, w_ffn_gate, w_ffn_up, w_ffn_down))
    vec3 = lambda v: v.reshape(depth, 1, v.shape[-1])
    subln3, pscale3 = vec3(subln_g), vec3(pool_scale)
    ln1_g3, ln1_b3, ln2_g3, ln2_b3 = map(vec3, (ln1_g, ln1_b, ln2_g, ln2_b))

    rope = _rope_tables(seq)
    cs_lat, wpos_lat = _dft_tables(seq)
    cs_ctx, wpos_ctx = _dft_tables(ctx_len)

    n_lat_rows = n_batch * seq
    ctx_blk0 = n_lat_rows // ctx_len

    for l in range(depth):
        lam_init = 0.8 - 0.6 * math.exp(-0.3 * l)
        p = _in_projection(xs, mod, w_in_b, rope, l, n_lat_rows // TM_IN, seq // TM_IN, n_batch)
        att = _attention(p, lam_qk, subln3, l, lam_init, n_batch, seq, ctx_len)
        four = _fourier(p, cs_lat, wpos_lat, seq, 0, n_batch)
        four = _fourier(p, cs_ctx, wpos_ctx, ctx_len, ctx_blk0, n_batch, prev=four)
        pool = _pool(p, w_grp_b, pscale3, l, seq, 0, n_batch)
        pool = _pool(p, w_grp_b, pscale3, l, ctx_len, ctx_blk0, n_batch, prev=pool)
        xs = _merge(att, four, pool, p, xs, mod, ln1_g3, ln1_b3, w_att_b, w_four_b, w_pool_b,
                    w_out_b, l, alpha, seq // TM_MIX, n_batch)
        xs = _ffn(xs, mod, ln2_g3, ln2_b3, w_g_b, w_u_b, w_d_b, l, alpha, seq // TM_FFN, n_batch)

    return xs[:n_lat_rows].reshape(n_batch, seq, d)
```

```python
import functools
import math

import jax
import jax.numpy as jnp
import numpy as np
from jax import lax
from jax.experimental import pallas as pl
from jax.experimental.pallas import tpu as pltpu

F32 = jnp.float32
BF16 = jnp.bfloat16

D_MODEL = 1024
GRID_W = 64
ATT_HEADS = 8
QK_DIM = 64
V_DIM = 128
ROPE_FREQS = 16
ROPE_BASE = 10000.0
FOURIER_GROUPS = 4
GROUP_C = 128
POOL_WINDOWS = (2, 4, 8, 16)
OFF_K, OFF_V, OFF_Q, OFF_F, OFF_P, OFF_G = 0, 1024, 2048, 3072, 3584, 4096
N_IN = 7168
LN_EPS = 1e-5
LOG2_E = math.log2(math.e)

LANES = 128
VMEM_LIMIT = 56 * 1024 * 1024

TM_IN = 1024
TN_IN = 1024
TC_IN = 256
ATT_TQ = 512
TM_MIX = 512
TM_FFN = 512
TR_FOUR = 512
POOL_T = 256
POOL_PAD = 128
MOD_ROWS = 16


def _cparams(sem):
    return pltpu.CompilerParams(dimension_semantics=sem, vmem_limit_bytes=VMEM_LIMIT)


def _layer_norm(x):
    mu = jnp.mean(x, axis=-1, keepdims=True)
    xc = x - mu
    var = jnp.mean(xc * xc, axis=-1, keepdims=True)
    return xc * lax.rsqrt(var + LN_EPS)


def _sigmoid(x):
    return 1.0 / (1.0 + jnp.exp(-x))


def _mod_kernel(c_ref, w_ref, b_ref, o_ref):
    c = c_ref[...]
    sc = (c * _sigmoid(c)).astype(BF16)
    o_ref[...] = jnp.dot(sc, w_ref[...].astype(BF16), preferred_element_type=F32) + b_ref[...]


def _mod_vectors(cond, w_mod, b_mod):
    depth, d, n = w_mod.shape
    tn = 1024
    return pl.pallas_call(
        _mod_kernel,
        grid=(depth, n // tn),
        in_specs=[
            pl.BlockSpec((MOD_ROWS, d), lambda l, j: (0, 0)),
            pl.BlockSpec((None, d, tn), lambda l, j: (l, 0, j)),
            pl.BlockSpec((None, 1, tn), lambda l, j: (l, 0, j)),
        ],
        out_specs=pl.BlockSpec((None, MOD_ROWS, tn), lambda l, j: (l, 0, j)),
        out_shape=jax.ShapeDtypeStruct((depth, MOD_ROWS, n), F32),
        compiler_params=_cparams(("arbitrary", "arbitrary")),
        name="mod_vectors",
    )(cond, w_mod, b_mod.reshape(depth, 1, n))


def _inproj_kernel(*refs, aliased):
    if aliased:
        x_ref, sh_ref, sc_ref, w_ref, cos_ref, sin_ref, _, o_ref, u_ref = refs
    else:
        x_ref, sh_ref, sc_ref, w_ref, cos_ref, sin_ref, o_ref, u_ref = refs
    j = pl.program_id(1)

    @pl.when(j == 0)
    def _():
        h = _layer_norm(x_ref[...])
        u_ref[...] = (h * (1.0 + sc_ref[...]) + sh_ref[...]).astype(BF16)

    is_k = j == OFF_K // TN_IN
    is_q = j == OFF_Q // TN_IN
    is_gate = j >= OFF_G // TN_IN

    def project(epilogue):
        for c in range(TN_IN // TC_IN):
            cols = slice(c * TC_IN, (c + 1) * TC_IN)
            acc = jnp.dot(u_ref[...], w_ref[:, cols], preferred_element_type=F32)
            epilogue(acc, cols)

    def rope(acc, cols):
        scale = jnp.where(is_q, QK_DIM ** -0.5 * LOG2_E, 1.0).astype(F32)
        cos, sin = cos_ref[...], sin_ref[...]
        for v in range(TC_IN // LANES):
            xs = acc[:, v * LANES:(v + 1) * LANES]
            r = xs * cos + pltpu.roll(xs, LANES // 2, 1) * sin
            lo = cols.start + v * LANES
            o_ref[:, lo:lo + LANES] = (r * scale).astype(BF16)

    def gate(acc, cols):
        o_ref[:, cols] = _sigmoid(acc).astype(BF16)

    def plain(acc, cols):
        o_ref[:, cols] = acc.astype(BF16)

    pl.when(is_k | is_q)(lambda: project(rope))
    pl.when(is_gate)(lambda: project(gate))
    pl.when(jnp.logical_not(is_k | is_q | is_gate))(lambda: project(plain))


def _in_projection(x, mod, w_in, rope, layer, n_lat_tiles, tiles_per_batch, n_batch,
                   row_tile0=0, n_row_tiles=None, n_col_tiles=None, prev=None):
    rows, d = x.shape
    cos, sin = rope
    n_rope_blocks = cos.shape[0] // TM_IN - 1
    n_row_tiles = rows // TM_IN if n_row_tiles is None else n_row_tiles
    n_col_tiles = N_IN // TN_IN if n_col_tiles is None else n_col_tiles

    def mod_idx(i):
        return layer * MOD_ROWS + jnp.minimum((i + row_tile0) // tiles_per_batch, n_batch)

    def rope_idx(i, j):
        it = i + row_tile0
        return (jnp.where(it < n_lat_tiles, it % tiles_per_batch, n_rope_blocks), 0)

    in_specs = [
        pl.BlockSpec((TM_IN, d), lambda i, j: (i + row_tile0, 0)),
        pl.BlockSpec((None, 1, d), lambda i, j: (mod_idx(i), 0, 0)),
        pl.BlockSpec((None, 1, d), lambda i, j: (mod_idx(i), 0, 1)),
        pl.BlockSpec((None, d, TN_IN), lambda i, j: (layer, 0, j)),
        pl.BlockSpec((TM_IN, LANES), rope_idx),
        pl.BlockSpec((TM_IN, LANES), rope_idx),
    ]
    args = [x, mod, mod, w_in, cos, sin]
    aliases = {}
    if prev is not None:
        in_specs.append(pl.BlockSpec(memory_space=pl.ANY))
        args.append(prev)
        aliases = {6: 0}
    return pl.pallas_call(
        functools.partial(_inproj_kernel, aliased=prev is not None),
        grid=(n_row_tiles, n_col_tiles),
        in_specs=in_specs,
        out_specs=pl.BlockSpec((TM_IN, TN_IN), lambda i, j: (i + row_tile0, j)),
        out_shape=jax.ShapeDtypeStruct((rows, N_IN), BF16),
        scratch_shapes=[pltpu.VMEM((TM_IN, d), BF16)],
        input_output_aliases=aliases,
        compiler_params=_cparams(("arbitrary", "arbitrary")),
        name="in_projection",
    )(*args)


_NT = (((1,), (1,)), ((), ()))


def _lam_value(lam_ref, lam_init):
    lq = lam_ref[...]
    return (jnp.exp(jnp.sum(lq[0:1] * lq[1:2], axis=1, keepdims=True))
            - jnp.exp(jnp.sum(lq[2:3] * lq[3:4], axis=1, keepdims=True)) + lam_init)


def _split_maps(q):
    lane = lax.broadcasted_iota(jnp.int32, q.shape, 1)
    first = (lane // (QK_DIM // 2)) % 2 == 0
    zero = jnp.zeros_like(q)
    return jnp.where(first, q, zero), jnp.where(first, zero, q)


def _head_norm(o, g_ref, lam_init):
    y = o * lax.rsqrt(jnp.mean(o * o, axis=-1, keepdims=True) + LN_EPS)
    return (y * g_ref[...] * (1.0 - lam_init)).astype(BF16)


def _attn_latent_kernel(lam_ref, g_ref, q_ref, kl_ref, kc_ref, vl_ref, vc_ref, o_ref,
                        s_even, m_even, s_odd, m_odd, vx_ref, *, lam_init, seq, ctx_len, n_q):
    t = pl.program_id(0)
    lam = _lam_value(lam_ref, lam_init)
    q1, q2 = _split_maps(q_ref[...])
    tq = q_ref.shape[0]

    @pl.when(t == 0)
    def _():
        s_odd[...] = jnp.zeros_like(s_odd)
        m_odd[...] = jnp.zeros_like(m_odd)
        vx_ref[:, V_DIM:2 * V_DIM] = jnp.ones((seq + ctx_len, V_DIM), BF16)

    @pl.when(jnp.maximum(t - 1, 0) % n_q == 0)
    def _():
        vx_ref[0:seq, 0:V_DIM] = vl_ref[...]
        vx_ref[seq:seq + ctx_len, 0:V_DIM] = vc_ref[...]

    def step(s_w, m_w, s_r, m_r):
        for mp, qm in enumerate((q1, q2)):
            sl = lax.dot_general(qm, kl_ref[...], _NT, preferred_element_type=F32)
            sc = lax.dot_general(qm, kc_ref[...], _NT, preferred_element_type=F32)
            m = jnp.maximum(jnp.max(sl, axis=-1, keepdims=True),
                            jnp.max(sc, axis=-1, keepdims=True))
            s_w[mp, :, 0:seq] = sl
            s_w[mp, :, seq:seq + ctx_len] = sc
            m_w[mp] = jnp.broadcast_to(m, (tq, LANES))
        e1 = jnp.exp2(s_r[0] - m_r[0][:, 0:1]).astype(BF16)
        e2 = jnp.exp2(s_r[1] - m_r[1][:, 0:1]).astype(BF16)
        r1 = jnp.dot(e1, vx_ref[...], preferred_element_type=F32)
        r2 = jnp.dot(e2, vx_ref[...], preferred_element_type=F32)
        o = (r1[:, 0:V_DIM] / r1[:, V_DIM:2 * V_DIM]
             - lam * (r2[:, 0:V_DIM] / r2[:, V_DIM:2 * V_DIM]))
        o_ref[...] = _head_norm(o, g_ref, lam_init)

    @pl.when(t % 2 == 0)
    def _():
        step(s_even, m_even, s_odd, m_odd)

    @pl.when(t % 2 == 1)
    def _():
        step(s_odd, m_odd, s_even, m_even)


def _attn_context_kernel(lam_ref, g_ref, q_ref, k_ref, v_ref, _, o_ref, *, lam_init):
    lam = _lam_value(lam_ref, lam_init)
    q1, q2 = _split_maps(q_ref[...])
    s1 = lax.dot_general(q1, k_ref[...], _NT, preferred_element_type=F32)
    s2 = lax.dot_general(q2, k_ref[...], _NT, preferred_element_type=F32)
    e1 = jnp.exp2(s1 - jnp.max(s1, axis=-1, keepdims=True))
    e2 = jnp.exp2(s2 - jnp.max(s2, axis=-1, keepdims=True))
    c1 = 1.0 / jnp.sum(e1, axis=-1, keepdims=True)
    c2 = lam / jnp.sum(e2, axis=-1, keepdims=True)
    a = (e1 * c1 - e2 * c2).astype(BF16)
    o = jnp.dot(a, v_ref[...], preferred_element_type=F32)
    o_ref[...] = _head_norm(o, g_ref, lam_init)


def _attention(p, lam_qk, subln_g, layer, lam_init, n_batch, seq, ctx_len, with_context):
    rows = p.shape[0]
    n_q = seq // ATT_TQ
    n_tiles = n_batch * ATT_HEADS * n_q
    ctx_blk = n_batch * seq // ctx_len
    kcol, vcol, qcol = OFF_K // LANES, OFF_V // LANES, OFF_Q // LANES
    width = ATT_HEADS * V_DIM

    def decode(tile):
        qi = tile % n_q
        bh = tile // n_q
        return bh // ATT_HEADS, bh % ATT_HEADS, qi

    def first(t):
        return decode(jnp.minimum(t, n_tiles - 1))

    def second(t):
        return decode(jnp.maximum(t - 1, 0))

    def q_idx(t):
        b, h, qi = first(t)
        return (b * n_q + qi, qcol + h)

    def o_idx(t):
        b, h, qi = second(t)
        return (b * n_q + qi, h)

    def kv_idx(which, col0, blk0):
        def idx(t):
            b, h, _ = which(t)
            return (blk0 + b, col0 + h)
        return idx

    lam_spec = pl.BlockSpec((None, 4, QK_DIM), lambda *_: (layer, 0, 0))
    g_spec = pl.BlockSpec((None, 1, V_DIM), lambda *_: (layer, 0, 0))
    s_shape = pltpu.VMEM((2, ATT_TQ, seq + ctx_len), F32)
    m_shape = pltpu.VMEM((2, ATT_TQ, LANES), F32)
    att = pl.pallas_call(
        functools.partial(_attn_latent_kernel, lam_init=lam_init, seq=seq, ctx_len=ctx_len,
                          n_q=n_q),
        grid=(n_tiles + 1,),
        in_specs=[
            lam_spec, g_spec,
            pl.BlockSpec((ATT_TQ, LANES), q_idx),
            pl.BlockSpec((seq, LANES), kv_idx(first, kcol, 0)),
            pl.BlockSpec((ctx_len, LANES), kv_idx(first, kcol, ctx_blk)),
            pl.BlockSpec((seq, LANES), kv_idx(second, vcol, 0)),
            pl.BlockSpec((ctx_len, LANES), kv_idx(second, vcol, ctx_blk)),
        ],
        out_specs=pl.BlockSpec((ATT_TQ, LANES), o_idx),
        out_shape=jax.ShapeDtypeStruct((rows, width), BF16),
        scratch_shapes=[s_shape, m_shape, s_shape, m_shape,
                        pltpu.VMEM((seq + ctx_len, 2 * V_DIM), BF16)],
        compiler_params=_cparams(("arbitrary",)),
        name="diff_attention",
    )(lam_qk, subln_g, p, p, p, p, p)
    if not with_context:
        return att

    return pl.pallas_call(
        functools.partial(_attn_context_kernel, lam_init=lam_init),
        grid=(n_batch, ATT_HEADS),
        in_specs=[
            lam_spec, g_spec,
            pl.BlockSpec((ctx_len, LANES), lambda b, h: (ctx_blk + b, qcol + h)),
            pl.BlockSpec((ctx_len, LANES), lambda b, h: (ctx_blk + b, kcol + h)),
            pl.BlockSpec((ctx_len, LANES), lambda b, h: (ctx_blk + b, vcol + h)),
            pl.BlockSpec(memory_space=pl.ANY),
        ],
        out_specs=pl.BlockSpec((ctx_len, LANES), lambda b, h: (ctx_blk + b, h)),
        out_shape=jax.ShapeDtypeStruct((rows, width), BF16),
        input_output_aliases={5: 0},
        compiler_params=_cparams(("arbitrary", "arbitrary")),
        name="diff_attention_ctx",
    )(lam_qk, subln_g, p, p, p, att)


def _fourier_kernel(*refs, length, norm, aliased):
    if aliased:
        x_ref, cs_ref, w_ref, _, o_ref, xcs_ref = refs
    else:
        x_ref, cs_ref, w_ref, o_ref, xcs_ref = refs

    @pl.when(pl.program_id(1) == 0)
    def _():
        for g in range(FOURIER_GROUPS):
            t = jnp.dot(x_ref[:, g * GROUP_C:(g + 1) * GROUP_C], cs_ref[...],
                        preferred_element_type=F32)
            xcs_ref[0:length, g * GROUP_C:(g + 1) * GROUP_C] = t[:, :GROUP_C].astype(BF16)
            xcs_ref[length:2 * length, g * GROUP_C:(g + 1) * GROUP_C] = t[:, GROUP_C:].astype(BF16)

    y = jnp.dot(w_ref[...], xcs_ref[...], preferred_element_type=F32)
    o_ref[...] = (y * norm).astype(BF16)


def _fourier(p, cs, w_pos, length, row_blk0, n_batch, prev=None):
    rows = p.shape[0]
    width = FOURIER_GROUPS * GROUP_C
    tr = min(TR_FOUR, length)
    n_r = length // tr
    norm = 1.0 / math.sqrt(length * GROUP_C)
    in_specs = [
        pl.BlockSpec((length, width), lambda b, r: (row_blk0 + b, OFF_F // width)),
        pl.BlockSpec((GROUP_C, 2 * GROUP_C), lambda b, r: (0, 0)),
        pl.BlockSpec((tr, 2 * length), lambda b, r: (r, 0)),
    ]
    args = [p, cs, w_pos]
    aliases = {}
    if prev is not None:
        in_specs.append(pl.BlockSpec(memory_space=pl.ANY))
        args.append(prev)
        aliases = {3: 0}
    kern = functools.partial(_fourier_kernel, length=length, norm=norm, aliased=prev is not None)
    return pl.pallas_call(
        kern,
        grid=(n_batch, n_r),
        in_specs=in_specs,
        out_specs=pl.BlockSpec((tr, width), lambda b, r: ((row_blk0 + b) * n_r + r, 0)),
        out_shape=jax.ShapeDtypeStruct((rows, width), BF16),
        scratch_shapes=[pltpu.VMEM((2 * length, width), BF16)],
        input_output_aliases=aliases,
        compiler_params=_cparams(("arbitrary", "arbitrary")),
        name=f"fourier_{length}",
    )(*args)


def _pool_kernel(*refs, length, aliased):
    if aliased:
        x_ref, band_ref, wg_ref, sc_ref, _, o_ref, xp_ref = refs
    else:
        x_ref, band_ref, wg_ref, sc_ref, o_ref, xp_ref = refs
    width = x_ref.shape[1]
    t_rows = min(POOL_T, length)
    slab_rows = t_rows + 2 * POOL_PAD
    xp_ref[0:POOL_PAD, :] = jnp.zeros((POOL_PAD, width), BF16)
    xp_ref[POOL_PAD + length:2 * POOL_PAD + length, :] = jnp.zeros((POOL_PAD, width), BF16)
    xp_ref[POOL_PAD:POOL_PAD + length, :] = x_ref[...]
    for i in range(length // t_rows):
        r0 = i * t_rows
        t = r0 + lax.broadcasted_iota(jnp.int32, (t_rows, 1), 0)
        for g, w in enumerate(POOL_WINDOWS):
            lo = w // 2
            hi = w - lo
            cols = slice(g * GROUP_C, (g + 1) * GROUP_C)
            slab = xp_ref[r0:r0 + slab_rows, cols]
            sums = jnp.dot(band_ref[g], slab, preferred_element_type=F32)
            cnt = (jnp.minimum(t + hi, length) - jnp.maximum(t - lo, 0)).astype(F32)
            d = sums / cnt - x_ref[r0:r0 + t_rows, cols].astype(F32)
            y = jnp.dot(d.astype(BF16), wg_ref[g], preferred_element_type=F32)
            o_ref[r0:r0 + t_rows, cols] = (y * sc_ref[:, cols]).astype(BF16)


def _pool_bands(t_rows):
    tl = np.arange(t_rows)[:, None] + POOL_PAD
    jl = np.arange(t_rows + 2 * POOL_PAD)[None, :]
    bands = []
    for w in POOL_WINDOWS:
        lo = w // 2
        hi = w - lo
        bands.append(((jl >= tl - lo) & (jl < tl + hi)).astype(np.float32))
    return jnp.asarray(np.stack(bands), dtype=BF16)


def _pool(p, w_grp, scale, layer, length, row_blk0, n_batch, prev=None):
    rows = p.shape[0]
    width = len(POOL_WINDOWS) * GROUP_C
    t_rows = min(POOL_T, length)
    bands = _pool_bands(t_rows)
    in_specs = [
        pl.BlockSpec((length, width), lambda b: (row_blk0 + b, OFF_P // width)),
        pl.BlockSpec(bands.shape, lambda b: (0, 0, 0)),
        pl.BlockSpec((None, len(POOL_WINDOWS), GROUP_C, GROUP_C), lambda b: (layer, 0, 0, 0)),
        pl.BlockSpec((None, 1, width), lambda b: (layer, 0, 0)),
    ]
    args = [p, bands, w_grp, scale]
    aliases = {}
    if prev is not None:
        in_specs.append(pl.BlockSpec(memory_space=pl.ANY))
        args.append(prev)
        aliases = {4: 0}
    kern = functools.partial(_pool_kernel, length=length, aliased=prev is not None)
    return pl.pallas_call(
        kern,
        grid=(n_batch,),
        in_specs=in_specs,
        out_specs=pl.BlockSpec((length, width), lambda b: (row_blk0 + b, 0)),
        out_shape=jax.ShapeDtypeStruct((rows, width), BF16),
        scratch_shapes=[pltpu.VMEM((length + 2 * POOL_PAD, width), BF16)],
        input_output_aliases=aliases,
        compiler_params=_cparams(("arbitrary",)),
        name=f"pool_{length}",
    )(*args)


def _merge_kernel(att_ref, four_ref, pool_ref, g0_ref, g1_ref, g2_ref, x_ref, gm_ref,
                  lng_ref, lnb_ref, wa_ref, wf_ref, wp_ref, wo_ref, o_ref, *, alpha):
    ba = jnp.dot(att_ref[...], wa_ref[...], preferred_element_type=F32)
    bf = jnp.dot(four_ref[...], wf_ref[...], preferred_element_type=F32)
    bp = jnp.dot(pool_ref[...], wp_ref[...], preferred_element_type=F32)
    m = (g0_ref[...].astype(F32) * ba + g1_ref[...].astype(F32) * bf
         + g2_ref[...].astype(F32) * bp)
    y = jnp.dot(m.astype(BF16), wo_ref[...], preferred_element_type=F32)
    z = alpha * x_ref[...] + gm_ref[...] * y
    o_ref[...] = _layer_norm(z) * lng_ref[...] + lnb_ref[...]


def _merge(att, four, pool, p, x, mod, ln_g, ln_b, w_att, w_four, w_pool, w_out, layer,
           alpha, tiles_per_batch, n_batch, rows):
    d = x.shape[1]
    tm = TM_MIX
    gcol = OFF_G // d

    def mod_idx(i):
        return layer * MOD_ROWS + jnp.minimum(i // tiles_per_batch, n_batch)

    def const3(i):
        return (layer, 0, 0)

    return pl.pallas_call(
        functools.partial(_merge_kernel, alpha=alpha),
        grid=(rows // tm,),
        in_specs=[
            pl.BlockSpec((tm, att.shape[1]), lambda i: (i, 0)),
            pl.BlockSpec((tm, four.shape[1]), lambda i: (i, 0)),
            pl.BlockSpec((tm, pool.shape[1]), lambda i: (i, 0)),
            pl.BlockSpec((tm, d), lambda i: (i, gcol)),
            pl.BlockSpec((tm, d), lambda i: (i, gcol + 1)),
            pl.BlockSpec((tm, d), lambda i: (i, gcol + 2)),
            pl.BlockSpec((tm, d), lambda i: (i, 0)),
            pl.BlockSpec((None, 1, d), lambda i: (mod_idx(i), 0, 2)),
            pl.BlockSpec((None, 1, d), const3),
            pl.BlockSpec((None, 1, d), const3),
            pl.BlockSpec((None,) + w_att.shape[1:], const3),
            pl.BlockSpec((None,) + w_four.shape[1:], const3),
            pl.BlockSpec((None,) + w_pool.shape[1:], const3),
            pl.BlockSpec((None,) + w_out.shape[1:], const3),
        ],
        out_specs=pl.BlockSpec((tm, d), lambda i: (i, 0)),
        out_shape=jax.ShapeDtypeStruct((rows, d), F32),
        compiler_params=_cparams(("arbitrary",)),
        name="branch_merge",
    )(att, four, pool, p, p, p, x, mod, ln_g, ln_b, w_att, w_four, w_pool, w_out)


def _ffn_kernel(x_ref, sh_ref, sc_ref, gm_ref, lng_ref, lnb_ref, wg_ref, wu_ref, wd_ref,
                o_ref, h_ref, acc_ref, *, alpha):
    j = pl.program_id(1)

    @pl.when(j == 0)
    def _():
        h = _layer_norm(x_ref[...])
        h_ref[...] = (h * (1.0 + sc_ref[...]) + sh_ref[...]).astype(BF16)
        acc_ref[...] = jnp.zeros_like(acc_ref)

    h = h_ref[...]
    gate = jnp.dot(h, wg_ref[...], preferred_element_type=F32)
    up = jnp.dot(h, wu_ref[...], preferred_element_type=F32)
    a = (gate * _sigmoid(gate) * up).astype(BF16)
    acc_ref[...] += jnp.dot(a, wd_ref[...], preferred_element_type=F32)

    @pl.when(j == pl.num_programs(1) - 1)
    def _():
        z = alpha * x_ref[...] + gm_ref[...] * acc_ref[...]
        o_ref[...] = _layer_norm(z) * lng_ref[...] + lnb_ref[...]


def _ffn(x, mod, ln_g, ln_b, w_gate, w_up, w_down, layer, alpha, tiles_per_batch, n_batch):
    rows, d = x.shape
    d_ff = w_gate.shape[2]
    tm = TM_FFN
    tf = d_ff // 2

    def mod_idx(i):
        return layer * MOD_ROWS + jnp.minimum(i // tiles_per_batch, n_batch)

    return pl.pallas_call(
        functools.partial(_ffn_kernel, alpha=alpha),
        grid=(rows // tm, d_ff // tf),
        in_specs=[
            pl.BlockSpec((tm, d), lambda i, j: (i, 0)),
            pl.BlockSpec((None, 1, d), lambda i, j: (mod_idx(i), 0, 3)),
            pl.BlockSpec((None, 1, d), lambda i, j: (mod_idx(i), 0, 4)),
            pl.BlockSpec((None, 1, d), lambda i, j: (mod_idx(i), 0, 5)),
            pl.BlockSpec((None, 1, d), lambda i, j: (layer, 0, 0)),
            pl.BlockSpec((None, 1, d), lambda i, j: (layer, 0, 0)),
            pl.BlockSpec((None, d, tf), lambda i, j: (layer, 0, j)),
            pl.BlockSpec((None, d, tf), lambda i, j: (layer, 0, j)),
            pl.BlockSpec((None, tf, d), lambda i, j: (layer, j, 0)),
        ],
        out_specs=pl.BlockSpec((tm, d), lambda i, j: (i, 0)),
        out_shape=jax.ShapeDtypeStruct((rows, d), F32),
        scratch_shapes=[pltpu.VMEM((tm, d), BF16), pltpu.VMEM((tm, d), F32)],
        compiler_params=_cparams(("arbitrary", "arbitrary")),
        name="swiglu",
    )(x, mod, mod, mod, ln_g, ln_b, w_gate, w_up, w_down)


def _rope_layout(w):
    lead = w.shape[:-1]
    w = w.reshape(lead + (ATT_HEADS, 2, 2, 2, ROPE_FREQS))
    nd = len(lead)
    perm = tuple(range(nd)) + (nd, nd + 3, nd + 1, nd + 2, nd + 4)
    return w.transpose(perm).reshape(lead + (ATT_HEADS * 2 * QK_DIM,))


def _rope_tables(seq):
    rows = seq // GRID_W
    row = np.repeat(np.arange(rows), GRID_W).astype(np.float32)
    col = np.tile(np.arange(GRID_W), rows).astype(np.float32)
    inv = (ROPE_BASE ** (-np.arange(ROPE_FREQS, dtype=np.float32) / ROPE_FREQS)).astype(np.float32)
    ar = row[:, None] * inv[None, :]
    ac = col[:, None] * inv[None, :]
    ang = np.concatenate([ar, ac, ar, ac], axis=-1).astype(np.float32)
    cos = np.cos(ang).astype(np.float32)
    sin = np.sin(ang).astype(np.float32)
    cos = np.concatenate([cos, cos], axis=-1)
    sin = np.concatenate([-sin, sin], axis=-1)

    def with_identity(t, fill):
        ident = np.full((TM_IN, LANES), fill, np.float32)
        return jnp.asarray(np.concatenate([t, ident], axis=0), dtype=F32)

    return with_identity(cos, 1.0), with_identity(sin, 0.0)


def _dft_tables(length):
    def cos_sin(n):
        k = np.arange(n, dtype=np.int64)
        ang = 2.0 * np.pi * ((k[:, None] * k[None, :]) % n).astype(np.float64) / n
        return np.cos(ang), np.sin(ang)

    cc, sc = cos_sin(GROUP_C)
    cl, sl = cos_sin(length)
    cs = jnp.asarray(np.concatenate([cc, sc], axis=1), dtype=F32).astype(BF16)
    w_pos = jnp.asarray(np.concatenate([cl, -sl], axis=1), dtype=F32).astype(BF16)
    return cs, w_pos


def kernel(x, c, ctx, c_ctx, w_mod, b_mod, w_in, lam_qk, subln_g, w_att_br, w_four_br,
           w_pool_grp, pool_scale, w_pool_br, w_out, ln1_g, ln1_b, w_ffn_gate, w_ffn_up,
           w_ffn_down, ln2_g, ln2_b):
    n_batch, seq, d = x.shape
    ctx_len = ctx.shape[1]
    depth = w_mod.shape[0]
    assert d == D_MODEL and seq % ATT_TQ == 0 and seq % TM_IN == 0 and n_batch < MOD_ROWS
    assert (n_batch * ctx_len) % TM_IN == 0 and seq % ctx_len == 0
    alpha = (2 * depth) ** 0.25

    xs = jnp.concatenate([x.reshape(n_batch * seq, d), ctx.reshape(n_batch * ctx_len, d)], axis=0)

    cond = jnp.zeros((MOD_ROWS, d), F32).at[:n_batch].set(c).at[n_batch].set(c_ctx)
    mod = _mod_vectors(cond, w_mod, b_mod).reshape(depth * MOD_ROWS, 1, 6 * d)

    bf = lambda w: w.astype(BF16)
    w_att_b, w_four_b, w_pool_b, w_out_b = map(bf, (w_att_br, w_four_br, w_pool_br, w_out))
    w_grp_b, w_g_b, w_u_b, w_d_b = map(bf, (w_pool_grp, w_ffn_gate, w_ffn_up, w_ffn_down))
    w_in_b = jnp.concatenate([_rope_layout(bf(w_in[..., OFF_K:OFF_V])), bf(w_in[..., OFF_V:OFF_Q]),
                              _rope_layout(bf(w_in[..., OFF_Q:OFF_F])), bf(w_in[..., OFF_F:])], axis=-1)
    vec3 = lambda v: v.reshape(depth, 1, v.shape[-1])
    subln3, pscale3 = vec3(subln_g), vec3(pool_scale)
    ln1_g3, ln1_b3, ln2_g3, ln2_b3 = map(vec3, (ln1_g, ln1_b, ln2_g, ln2_b))

    rope = _rope_tables(seq)
    cs_lat, wpos_lat = _dft_tables(seq)
    cs_ctx, wpos_ctx = _dft_tables(ctx_len)

    n_lat_rows = n_batch * seq
    ctx_blk0 = n_lat_rows // ctx_len

    n_lat_tiles = n_lat_rows // TM_IN
    for l in range(depth):
        last = l == depth - 1
        lam_init = 0.8 - 0.6 * math.exp(-0.3 * l)
        proj = functools.partial(_in_projection, xs, mod, w_in_b, rope, l, n_lat_tiles,
                                 seq // TM_IN, n_batch)
        if last:
            p = proj(n_row_tiles=n_lat_tiles)
            p = proj(row_tile0=n_lat_tiles, n_row_tiles=xs.shape[0] // TM_IN - n_lat_tiles,
                     n_col_tiles=OFF_Q // TN_IN, prev=p)
        else:
            p = proj()
        att = _attention(p, lam_qk, subln3, l, lam_init, n_batch, seq, ctx_len,
                         with_context=not last)
        four = _fourier(p, cs_lat, wpos_lat, seq, 0, n_batch)
        pool = _pool(p, w_grp_b, pscale3, l, seq, 0, n_batch)
        if not last:
            four = _fourier(p, cs_ctx, wpos_ctx, ctx_len, ctx_blk0, n_batch, prev=four)
            pool = _pool(p, w_grp_b, pscale3, l, ctx_len, ctx_blk0, n_batch, prev=pool)
        xs = _merge(att, four, pool, p, xs, mod, ln1_g3, ln1_b3, w_att_b, w_four_b, w_pool_b,
                    w_out_b, l, alpha, seq // TM_MIX, n_batch,
                    rows=n_lat_rows if last else xs.shape[0])
        xs = _ffn(xs, mod, ln2_g3, ln2_b3, w_g_b, w_u_b, w_d_b, l, alpha, seq // TM_FFN, n_batch)

    return xs.reshape(n_batch, seq, d)
```

```python
import functools
import math

import jax
import jax.numpy as jnp
import numpy as np
from jax import lax
from jax.experimental import pallas as pl
from jax.experimental.pallas import tpu as pltpu

F32 = jnp.float32
BF16 = jnp.bfloat16

D_MODEL = 1024
GRID_W = 64
ATT_HEADS = 8
QK_DIM = 64
V_DIM = 128
ROPE_FREQS = 16
ROPE_BASE = 10000.0
FOURIER_GROUPS = 4
GROUP_C = 128
POOL_WINDOWS = (2, 4, 8, 16)
OFF_K, OFF_V, OFF_Q, OFF_F, OFF_P, OFF_G = 0, 1024, 2048, 3072, 3584, 4096
N_IN = 7168
LN_EPS = 1e-5
LOG2_E = math.log2(math.e)

LANES = 128
VMEM_LIMIT = 56 * 1024 * 1024

TM_IN = 1024
TN_IN = 1024
TC_IN = 256
ATT_TQ = 512
TM_MIX = 512
TM_FFN = 512
TF_FFN = 256
TR_FOUR = 512
POOL_T = 256
POOL_PAD = 128
MOD_ROWS = 16


def _cparams(sem):
    return pltpu.CompilerParams(dimension_semantics=sem, vmem_limit_bytes=VMEM_LIMIT)


def _layer_norm(x):
    mu = jnp.mean(x, axis=-1, keepdims=True)
    xc = x - mu
    var = jnp.mean(xc * xc, axis=-1, keepdims=True)
    return xc * lax.rsqrt(var + LN_EPS)


def _sigmoid(x):
    return 1.0 / (1.0 + jnp.exp(-x))


def _mod_kernel(c_ref, w_ref, b_ref, o_ref):
    c = c_ref[...]
    sc = (c * _sigmoid(c)).astype(BF16)
    o_ref[...] = jnp.dot(sc, w_ref[...].astype(BF16), preferred_element_type=F32) + b_ref[...]


def _mod_vectors(cond, w_mod, b_mod):
    depth, d, n = w_mod.shape
    tn = 1024
    return pl.pallas_call(
        _mod_kernel,
        grid=(depth, n // tn),
        in_specs=[
            pl.BlockSpec((MOD_ROWS, d), lambda l, j: (0, 0)),
            pl.BlockSpec((None, d, tn), lambda l, j: (l, 0, j)),
            pl.BlockSpec((None, 1, tn), lambda l, j: (l, 0, j)),
        ],
        out_specs=pl.BlockSpec((None, MOD_ROWS, tn), lambda l, j: (l, 0, j)),
        out_shape=jax.ShapeDtypeStruct((depth, MOD_ROWS, n), F32),
        compiler_params=_cparams(("arbitrary", "arbitrary")),
        name="mod_vectors",
    )(cond, w_mod, b_mod.reshape(depth, 1, n))


def _inproj_kernel(*refs, aliased):
    if aliased:
        x_ref, sh_ref, sc_ref, w_ref, wkq_ref, cos_ref, sin_ref, _, o_ref, u_ref = refs
    else:
        x_ref, sh_ref, sc_ref, w_ref, wkq_ref, cos_ref, sin_ref, o_ref, u_ref = refs
    j = pl.program_id(1)

    @pl.when(j == 0)
    def _():
        h = _layer_norm(x_ref[...])
        u_ref[...] = (h * (1.0 + sc_ref[...]) + sh_ref[...]).astype(BF16)

    is_k = j == OFF_K // TN_IN
    is_q = j == OFF_Q // TN_IN
    is_gate = j >= OFF_G // TN_IN

    def project(epilogue, weights):
        for c in range(TN_IN // TC_IN):
            cols = slice(c * TC_IN, (c + 1) * TC_IN)
            acc = jnp.dot(u_ref[...], weights[:, cols], preferred_element_type=F32)
            epilogue(acc, cols)

    def rope(acc, cols):
        scale = jnp.where(is_q, QK_DIM ** -0.5 * LOG2_E, 1.0).astype(F32)
        cos, sin = cos_ref[...], sin_ref[...]
        for v in range(TC_IN // LANES):
            xs = acc[:, v * LANES:(v + 1) * LANES]
            r = xs * cos + pltpu.roll(xs, LANES // 2, 1) * sin
            lo = cols.start + v * LANES
            o_ref[:, lo:lo + LANES] = (r * scale).astype(BF16)

    def gate(acc, cols):
        o_ref[:, cols] = _sigmoid(acc).astype(BF16)

    def plain(acc, cols):
        o_ref[:, cols] = acc.astype(BF16)

    pl.when(is_k | is_q)(lambda: project(rope, wkq_ref))
    pl.when(is_gate)(lambda: project(gate, w_ref))
    pl.when(jnp.logical_not(is_k | is_q | is_gate))(lambda: project(plain, w_ref))


def _in_projection(x, mod, w_in, w_kq, rope, layer, n_lat_tiles, tiles_per_batch, n_batch,
                   row_tile0=0, n_row_tiles=None, n_col_tiles=None, prev=None):
    rows, d = x.shape
    cos, sin = rope
    n_rope_blocks = cos.shape[0] // TM_IN - 1
    n_row_tiles = rows // TM_IN if n_row_tiles is None else n_row_tiles
    n_col_tiles = N_IN // TN_IN if n_col_tiles is None else n_col_tiles

    def mod_idx(i):
        return layer * MOD_ROWS + jnp.minimum((i + row_tile0) // tiles_per_batch, n_batch)

    def rope_idx(i, j):
        it = i + row_tile0
        return (jnp.where(it < n_lat_tiles, it % tiles_per_batch, n_rope_blocks), 0)

    in_specs = [
        pl.BlockSpec((TM_IN, d), lambda i, j: (i + row_tile0, 0)),
        pl.BlockSpec((None, 1, d), lambda i, j: (mod_idx(i), 0, 0)),
        pl.BlockSpec((None, 1, d), lambda i, j: (mod_idx(i), 0, 1)),
        pl.BlockSpec((None, d, TN_IN), lambda i, j: (layer, 0, j)),
        pl.BlockSpec((None, d, TN_IN),
                     lambda i, j: (layer, 0, jnp.where(j >= OFF_Q // TN_IN, 1, 0))),
        pl.BlockSpec((TM_IN, LANES), rope_idx),
        pl.BlockSpec((TM_IN, LANES), rope_idx),
    ]
    args = [x, mod, mod, w_in, w_kq, cos, sin]
    aliases = {}
    if prev is not None:
        in_specs.append(pl.BlockSpec(memory_space=pl.ANY))
        args.append(prev)
        aliases = {len(args) - 1: 0}
    return pl.pallas_call(
        functools.partial(_inproj_kernel, aliased=prev is not None),
        grid=(n_row_tiles, n_col_tiles),
        in_specs=in_specs,
        out_specs=pl.BlockSpec((TM_IN, TN_IN), lambda i, j: (i + row_tile0, j)),
        out_shape=jax.ShapeDtypeStruct((rows, N_IN), BF16),
        scratch_shapes=[pltpu.VMEM((TM_IN, d), BF16)],
        input_output_aliases=aliases,
        compiler_params=_cparams(("arbitrary", "arbitrary")),
        name="in_projection",
    )(*args)


_NT = (((1,), (1,)), ((), ()))


def _lam_value(lam_ref, lam_init):
    lq = lam_ref[...]
    return (jnp.exp(jnp.sum(lq[0:1] * lq[1:2], axis=1, keepdims=True))
            - jnp.exp(jnp.sum(lq[2:3] * lq[3:4], axis=1, keepdims=True)) + lam_init)


def _split_maps(q):
    lane = lax.broadcasted_iota(jnp.int32, q.shape, 1)
    first = (lane // (QK_DIM // 2)) % 2 == 0
    zero = jnp.zeros_like(q)
    return jnp.where(first, q, zero), jnp.where(first, zero, q)


def _head_norm(o, g_ref, lam_init):
    y = o * lax.rsqrt(jnp.mean(o * o, axis=-1, keepdims=True) + LN_EPS)
    return (y * g_ref[...] * (1.0 - lam_init)).astype(BF16)


def _attn_latent_kernel(lam_ref, g_ref, q_ref, kl_ref, kc_ref, vl_ref, vc_ref, o_ref,
                        s_even, m_even, s_odd, m_odd, vx_ref, *, lam_init, seq, ctx_len, n_q):
    t = pl.program_id(0)
    lam = _lam_value(lam_ref, lam_init)
    q1, q2 = _split_maps(q_ref[...])
    tq = q_ref.shape[0]

    @pl.when(t == 0)
    def _():
        s_odd[...] = jnp.zeros_like(s_odd)
        m_odd[...] = jnp.zeros_like(m_odd)
        vx_ref[:, V_DIM:2 * V_DIM] = jnp.ones((seq + ctx_len, V_DIM), BF16)

    @pl.when(jnp.maximum(t - 1, 0) % n_q == 0)
    def _():
        vx_ref[0:seq, 0:V_DIM] = vl_ref[...]
        vx_ref[seq:seq + ctx_len, 0:V_DIM] = vc_ref[...]

    def step(s_w, m_w, s_r, m_r):
        for mp, qm in enumerate((q1, q2)):
            sl = lax.dot_general(qm, kl_ref[...], _NT, preferred_element_type=F32)
            sc = lax.dot_general(qm, kc_ref[...], _NT, preferred_element_type=F32)
            m = jnp.maximum(jnp.max(sl, axis=-1, keepdims=True),
                            jnp.max(sc, axis=-1, keepdims=True))
            s_w[mp, :, 0:seq] = sl
            s_w[mp, :, seq:seq + ctx_len] = sc
            m_w[mp] = jnp.broadcast_to(m, (tq, LANES))
        e1 = jnp.exp2(s_r[0] - m_r[0][:, 0:1]).astype(BF16)
        e2 = jnp.exp2(s_r[1] - m_r[1][:, 0:1]).astype(BF16)
        r1 = jnp.dot(e1, vx_ref[...], preferred_element_type=F32)
        r2 = jnp.dot(e2, vx_ref[...], preferred_element_type=F32)
        o = (r1[:, 0:V_DIM] / r1[:, V_DIM:2 * V_DIM]
             - lam * (r2[:, 0:V_DIM] / r2[:, V_DIM:2 * V_DIM]))
        o_ref[...] = _head_norm(o, g_ref, lam_init)

    @pl.when(t % 2 == 0)
    def _():
        step(s_even, m_even, s_odd, m_odd)

    @pl.when(t % 2 == 1)
    def _():
        step(s_odd, m_odd, s_even, m_even)


def _attn_context_kernel(lam_ref, g_ref, q_ref, k_ref, v_ref, _, o_ref, *, lam_init):
    lam = _lam_value(lam_ref, lam_init)
    q1, q2 = _split_maps(q_ref[...])
    s1 = lax.dot_general(q1, k_ref[...], _NT, preferred_element_type=F32)
    s2 = lax.dot_general(q2, k_ref[...], _NT, preferred_element_type=F32)
    e1 = jnp.exp2(s1 - jnp.max(s1, axis=-1, keepdims=True))
    e2 = jnp.exp2(s2 - jnp.max(s2, axis=-1, keepdims=True))
    c1 = 1.0 / jnp.sum(e1, axis=-1, keepdims=True)
    c2 = lam / jnp.sum(e2, axis=-1, keepdims=True)
    a = (e1 * c1 - e2 * c2).astype(BF16)
    o = jnp.dot(a, v_ref[...], preferred_element_type=F32)
    o_ref[...] = _head_norm(o, g_ref, lam_init)


def _attention(p, lam_qk, subln_g, layer, lam_init, n_batch, seq, ctx_len, with_context):
    rows = p.shape[0]
    n_q = seq // ATT_TQ
    n_tiles = n_batch * ATT_HEADS * n_q
    ctx_blk = n_batch * seq // ctx_len
    kcol, vcol, qcol = OFF_K // LANES, OFF_V // LANES, OFF_Q // LANES
    width = ATT_HEADS * V_DIM

    def decode(tile):
        qi = tile % n_q
        bh = tile // n_q
        return bh // ATT_HEADS, bh % ATT_HEADS, qi

    def first(t):
        return decode(jnp.minimum(t, n_tiles - 1))

    def second(t):
        return decode(jnp.maximum(t - 1, 0))

    def q_idx(t):
        b, h, qi = first(t)
        return (b * n_q + qi, qcol + h)

    def o_idx(t):
        b, h, qi = second(t)
        return (b * n_q + qi, h)

    def kv_idx(which, col0, blk0):
        def idx(t):
            b, h, _ = which(t)
            return (blk0 + b, col0 + h)
        return idx

    lam_spec = pl.BlockSpec((None, 4, QK_DIM), lambda *_: (layer, 0, 0))
    g_spec = pl.BlockSpec((None, 1, V_DIM), lambda *_: (layer, 0, 0))
    s_shape = pltpu.VMEM((2, ATT_TQ, seq + ctx_len), F32)
    m_shape = pltpu.VMEM((2, ATT_TQ, LANES), F32)
    att = pl.pallas_call(
        functools.partial(_attn_latent_kernel, lam_init=lam_init, seq=seq, ctx_len=ctx_len,
                          n_q=n_q),
        grid=(n_tiles + 1,),
        in_specs=[
            lam_spec, g_spec,
            pl.BlockSpec((ATT_TQ, LANES), q_idx),
            pl.BlockSpec((seq, LANES), kv_idx(first, kcol, 0)),
            pl.BlockSpec((ctx_len, LANES), kv_idx(first, kcol, ctx_blk)),
            pl.BlockSpec((seq, LANES), kv_idx(second, vcol, 0)),
            pl.BlockSpec((ctx_len, LANES), kv_idx(second, vcol, ctx_blk)),
        ],
        out_specs=pl.BlockSpec((ATT_TQ, LANES), o_idx),
        out_shape=jax.ShapeDtypeStruct((rows, width), BF16),
        scratch_shapes=[s_shape, m_shape, s_shape, m_shape,
                        pltpu.VMEM((seq + ctx_len, 2 * V_DIM), BF16)],
        compiler_params=_cparams(("arbitrary",)),
        name="diff_attention",
    )(lam_qk, subln_g, p, p, p, p, p)
    if not with_context:
        return att

    return pl.pallas_call(
        functools.partial(_attn_context_kernel, lam_init=lam_init),
        grid=(n_batch, ATT_HEADS),
        in_specs=[
            lam_spec, g_spec,
            pl.BlockSpec((ctx_len, LANES), lambda b, h: (ctx_blk + b, qcol + h)),
            pl.BlockSpec((ctx_len, LANES), lambda b, h: (ctx_blk + b, kcol + h)),
            pl.BlockSpec((ctx_len, LANES), lambda b, h: (ctx_blk + b, vcol + h)),
            pl.BlockSpec(memory_space=pl.ANY),
        ],
        out_specs=pl.BlockSpec((ctx_len, LANES), lambda b, h: (ctx_blk + b, h)),
        out_shape=jax.ShapeDtypeStruct((rows, width), BF16),
        input_output_aliases={5: 0},
        compiler_params=_cparams(("arbitrary", "arbitrary")),
        name="diff_attention_ctx",
    )(lam_qk, subln_g, p, p, p, att)


def _fourier_kernel(*refs, length, norm, aliased):
    if aliased:
        x_ref, cs_ref, w_ref, _, o_ref, xcs_ref = refs
    else:
        x_ref, cs_ref, w_ref, o_ref, xcs_ref = refs

    @pl.when(pl.program_id(1) == 0)
    def _():
        for g in range(FOURIER_GROUPS):
            t = jnp.dot(x_ref[:, g * GROUP_C:(g + 1) * GROUP_C], cs_ref[...],
                        preferred_element_type=F32)
            xcs_ref[0:length, g * GROUP_C:(g + 1) * GROUP_C] = t[:, :GROUP_C].astype(BF16)
            xcs_ref[length:2 * length, g * GROUP_C:(g + 1) * GROUP_C] = t[:, GROUP_C:].astype(BF16)

    y = jnp.dot(w_ref[...], xcs_ref[...], preferred_element_type=F32)
    o_ref[...] = (y * norm).astype(BF16)


def _fourier(p, cs, w_pos, length, row_blk0, n_batch, prev=None):
    rows = p.shape[0]
    width = FOURIER_GROUPS * GROUP_C
    tr = min(TR_FOUR, length)
    n_r = length // tr
    norm = 1.0 / math.sqrt(length * GROUP_C)
    in_specs = [
        pl.BlockSpec((length, width), lambda b, r: (row_blk0 + b, OFF_F // width)),
        pl.BlockSpec((GROUP_C, 2 * GROUP_C), lambda b, r: (0, 0)),
        pl.BlockSpec((tr, 2 * length), lambda b, r: (r, 0)),
    ]
    args = [p, cs, w_pos]
    aliases = {}
    if prev is not None:
        in_specs.append(pl.BlockSpec(memory_space=pl.ANY))
        args.append(prev)
        aliases = {3: 0}
    kern = functools.partial(_fourier_kernel, length=length, norm=norm, aliased=prev is not None)
    return pl.pallas_call(
        kern,
        grid=(n_batch, n_r),
        in_specs=in_specs,
        out_specs=pl.BlockSpec((tr, width), lambda b, r: ((row_blk0 + b) * n_r + r, 0)),
        out_shape=jax.ShapeDtypeStruct((rows, width), BF16),
        scratch_shapes=[pltpu.VMEM((2 * length, width), BF16)],
        input_output_aliases=aliases,
        compiler_params=_cparams(("arbitrary", "arbitrary")),
        name=f"fourier_{length}",
    )(*args)


def _pool_kernel(*refs, length, aliased):
    if aliased:
        x_ref, band_ref, wg_ref, sc_ref, _, o_ref, xp_ref = refs
    else:
        x_ref, band_ref, wg_ref, sc_ref, o_ref, xp_ref = refs
    width = x_ref.shape[1]
    t_rows = min(POOL_T, length)
    slab_rows = t_rows + 2 * POOL_PAD
    xp_ref[0:POOL_PAD, :] = jnp.zeros((POOL_PAD, width), BF16)
    xp_ref[POOL_PAD + length:2 * POOL_PAD + length, :] = jnp.zeros((POOL_PAD, width), BF16)
    xp_ref[POOL_PAD:POOL_PAD + length, :] = x_ref[...]
    for i in range(length // t_rows):
        r0 = i * t_rows
        t = r0 + lax.broadcasted_iota(jnp.int32, (t_rows, 1), 0)
        for g, w in enumerate(POOL_WINDOWS):
            lo = w // 2
            hi = w - lo
            cols = slice(g * GROUP_C, (g + 1) * GROUP_C)
            slab = xp_ref[r0:r0 + slab_rows, cols]
            sums = jnp.dot(band_ref[g], slab, preferred_element_type=F32)
            cnt = (jnp.minimum(t + hi, length) - jnp.maximum(t - lo, 0)).astype(F32)
            d = sums / cnt - x_ref[r0:r0 + t_rows, cols].astype(F32)
            y = jnp.dot(d.astype(BF16), wg_ref[g], preferred_element_type=F32)
            o_ref[r0:r0 + t_rows, cols] = (y * sc_ref[:, cols]).astype(BF16)


def _pool_bands(t_rows):
    tl = np.arange(t_rows)[:, None] + POOL_PAD
    jl = np.arange(t_rows + 2 * POOL_PAD)[None, :]
    bands = []
    for w in POOL_WINDOWS:
        lo = w // 2
        hi = w - lo
        bands.append(((jl >= tl - lo) & (jl < tl + hi)).astype(np.float32))
    return jnp.asarray(np.stack(bands), dtype=BF16)


def _pool(p, w_grp, scale, layer, length, row_blk0, n_batch, prev=None):
    rows = p.shape[0]
    width = len(POOL_WINDOWS) * GROUP_C
    t_rows = min(POOL_T, length)
    bands = _pool_bands(t_rows)
    in_specs = [
        pl.BlockSpec((length, width), lambda b: (row_blk0 + b, OFF_P // width)),
        pl.BlockSpec(bands.shape, lambda b: (0, 0, 0)),
        pl.BlockSpec((None, len(POOL_WINDOWS), GROUP_C, GROUP_C), lambda b: (layer, 0, 0, 0)),
        pl.BlockSpec((None, 1, width), lambda b: (layer, 0, 0)),
    ]
    args = [p, bands, w_grp, scale]
    aliases = {}
    if prev is not None:
        in_specs.append(pl.BlockSpec(memory_space=pl.ANY))
        args.append(prev)
        aliases = {4: 0}
    kern = functools.partial(_pool_kernel, length=length, aliased=prev is not None)
    return pl.pallas_call(
        kern,
        grid=(n_batch,),
        in_specs=in_specs,
        out_specs=pl.BlockSpec((length, width), lambda b: (row_blk0 + b, 0)),
        out_shape=jax.ShapeDtypeStruct((rows, width), BF16),
        scratch_shapes=[pltpu.VMEM((length + 2 * POOL_PAD, width), BF16)],
        input_output_aliases=aliases,
        compiler_params=_cparams(("arbitrary",)),
        name=f"pool_{length}",
    )(*args)


def _merge_kernel(att_ref, four_ref, pool_ref, g0_ref, g1_ref, g2_ref, x_ref, gm_ref,
                  lng_ref, lnb_ref, wa_ref, wf_ref, wp_ref, wo_ref, o_ref, *, alpha):
    ba = jnp.dot(att_ref[...], wa_ref[...], preferred_element_type=F32)
    bf = jnp.dot(four_ref[...], wf_ref[...], preferred_element_type=F32)
    bp = jnp.dot(pool_ref[...], wp_ref[...], preferred_element_type=F32)
    m = (g0_ref[...].astype(F32) * ba + g1_ref[...].astype(F32) * bf
         + g2_ref[...].astype(F32) * bp)
    y = jnp.dot(m.astype(BF16), wo_ref[...], preferred_element_type=F32)
    z = alpha * x_ref[...] + gm_ref[...] * y
    o_ref[...] = _layer_norm(z) * lng_ref[...] + lnb_ref[...]


def _merge(att, four, pool, p, x, mod, ln_g, ln_b, w_att, w_four, w_pool, w_out, layer,
           alpha, tiles_per_batch, n_batch, rows):
    d = x.shape[1]
    tm = TM_MIX
    gcol = OFF_G // d

    def mod_idx(i):
        return layer * MOD_ROWS + jnp.minimum(i // tiles_per_batch, n_batch)

    def const3(i):
        return (layer, 0, 0)

    return pl.pallas_call(
        functools.partial(_merge_kernel, alpha=alpha),
        grid=(rows // tm,),
        in_specs=[
            pl.BlockSpec((tm, att.shape[1]), lambda i: (i, 0)),
            pl.BlockSpec((tm, four.shape[1]), lambda i: (i, 0)),
            pl.BlockSpec((tm, pool.shape[1]), lambda i: (i, 0)),
            pl.BlockSpec((tm, d), lambda i: (i, gcol)),
            pl.BlockSpec((tm, d), lambda i: (i, gcol + 1)),
            pl.BlockSpec((tm, d), lambda i: (i, gcol + 2)),
            pl.BlockSpec((tm, d), lambda i: (i, 0)),
            pl.BlockSpec((None, 1, d), lambda i: (mod_idx(i), 0, 2)),
            pl.BlockSpec((None, 1, d), const3),
            pl.BlockSpec((None, 1, d), const3),
            pl.BlockSpec((None,) + w_att.shape[1:], const3),
            pl.BlockSpec((None,) + w_four.shape[1:], const3),
            pl.BlockSpec((None,) + w_pool.shape[1:], const3),
            pl.BlockSpec((None,) + w_out.shape[1:], const3),
        ],
        out_specs=pl.BlockSpec((tm, d), lambda i: (i, 0)),
        out_shape=jax.ShapeDtypeStruct((rows, d), F32),
        compiler_params=_cparams(("arbitrary",)),
        name="branch_merge",
    )(att, four, pool, p, p, p, x, mod, ln_g, ln_b, w_att, w_four, w_pool, w_out)


def _ffn_kernel(x_ref, sh_ref, sc_ref, gm_ref, lng_ref, lnb_ref, wg_ref, wu_ref, wd_ref,
                o_ref, *, alpha):
    x = x_ref[...]
    h = (_layer_norm(x) * (1.0 + sc_ref[...]) + sh_ref[...]).astype(BF16)
    d_ff = wg_ref.shape[1]
    acc = None
    for c in range(d_ff // TF_FFN):
        cols = slice(c * TF_FFN, (c + 1) * TF_FFN)
        gate = jnp.dot(h, wg_ref[:, cols], preferred_element_type=F32)
        up = jnp.dot(h, wu_ref[:, cols], preferred_element_type=F32)
        a = (gate * _sigmoid(gate) * up).astype(BF16)
        part = jnp.dot(a, wd_ref[cols, :], preferred_element_type=F32)
        acc = part if acc is None else acc + part
    z = alpha * x + gm_ref[...] * acc
    o_ref[...] = _layer_norm(z) * lng_ref[...] + lnb_ref[...]


def _ffn(x, mod, ln_g, ln_b, w_gate, w_up, w_down, layer, alpha, tiles_per_batch, n_batch):
    rows, d = x.shape
    d_ff = w_gate.shape[2]
    tm = TM_FFN
    assert d_ff % TF_FFN == 0

    def mod_idx(i):
        return layer * MOD_ROWS + jnp.minimum(i // tiles_per_batch, n_batch)

    def resident(shape):
        return pl.BlockSpec((None,) + shape, lambda i: (layer, 0, 0),
                            pipeline_mode=pl.Buffered(1))

    return pl.pallas_call(
        functools.partial(_ffn_kernel, alpha=alpha),
        grid=(rows // tm,),
        in_specs=[
            pl.BlockSpec((tm, d), lambda i: (i, 0)),
            pl.BlockSpec((None, 1, d), lambda i: (mod_idx(i), 0, 3)),
            pl.BlockSpec((None, 1, d), lambda i: (mod_idx(i), 0, 4)),
            pl.BlockSpec((None, 1, d), lambda i: (mod_idx(i), 0, 5)),
            pl.BlockSpec((None, 1, d), lambda i: (layer, 0, 0)),
            pl.BlockSpec((None, 1, d), lambda i: (layer, 0, 0)),
            resident((d, d_ff)),
            resident((d, d_ff)),
            resident((d_ff, d)),
        ],
        out_specs=pl.BlockSpec((tm, d), lambda i: (i, 0)),
        out_shape=jax.ShapeDtypeStruct((rows, d), F32),
        compiler_params=_cparams(("arbitrary",)),
        name="swiglu",
    )(x, mod, mod, mod, ln_g, ln_b, w_gate, w_up, w_down)


def _rope_layout(w):
    lead = w.shape[:-1]
    w = w.reshape(lead + (ATT_HEADS, 2, 2, 2, ROPE_FREQS))
    nd = len(lead)
    perm = tuple(range(nd)) + (nd, nd + 3, nd + 1, nd + 2, nd + 4)
    return w.transpose(perm).reshape(lead + (ATT_HEADS * 2 * QK_DIM,))


def _rope_tables(seq):
    rows = seq // GRID_W
    row = np.repeat(np.arange(rows), GRID_W).astype(np.float32)
    col = np.tile(np.arange(GRID_W), rows).astype(np.float32)
    inv = (ROPE_BASE ** (-np.arange(ROPE_FREQS, dtype=np.float32) / ROPE_FREQS)).astype(np.float32)
    ar = row[:, None] * inv[None, :]
    ac = col[:, None] * inv[None, :]
    ang = np.concatenate([ar, ac, ar, ac], axis=-1).astype(np.float32)
    cos = np.cos(ang).astype(np.float32)
    sin = np.sin(ang).astype(np.float32)
    cos = np.concatenate([cos, cos], axis=-1)
    sin = np.concatenate([-sin, sin], axis=-1)

    def with_identity(t, fill):
        ident = np.full((TM_IN, LANES), fill, np.float32)
        return jnp.asarray(np.concatenate([t, ident], axis=0), dtype=F32)

    return with_identity(cos, 1.0), with_identity(sin, 0.0)


def _dft_tables(length):
    def cos_sin(n):
        k = np.arange(n, dtype=np.int64)
        ang = 2.0 * np.pi * ((k[:, None] * k[None, :]) % n).astype(np.float64) / n
        return np.cos(ang), np.sin(ang)

    cc, sc = cos_sin(GROUP_C)
    cl, sl = cos_sin(length)
    cs = jnp.asarray(np.concatenate([cc, sc], axis=1), dtype=F32).astype(BF16)
    w_pos = jnp.asarray(np.concatenate([cl, -sl], axis=1), dtype=F32).astype(BF16)
    return cs, w_pos


def kernel(x, c, ctx, c_ctx, w_mod, b_mod, w_in, lam_qk, subln_g, w_att_br, w_four_br,
           w_pool_grp, pool_scale, w_pool_br, w_out, ln1_g, ln1_b, w_ffn_gate, w_ffn_up,
           w_ffn_down, ln2_g, ln2_b):
    n_batch, seq, d = x.shape
    ctx_len = ctx.shape[1]
    depth = w_mod.shape[0]
    assert d == D_MODEL and seq % ATT_TQ == 0 and seq % TM_IN == 0 and n_batch < MOD_ROWS
    assert (n_batch * ctx_len) % TM_IN == 0 and seq % ctx_len == 0
    alpha = (2 * depth) ** 0.25

    xs = jnp.concatenate([x.reshape(n_batch * seq, d), ctx.reshape(n_batch * ctx_len, d)], axis=0)

    cond = jnp.zeros((MOD_ROWS, d), F32).at[:n_batch].set(c).at[n_batch].set(c_ctx)
    mod = _mod_vectors(cond, w_mod, b_mod).reshape(depth * MOD_ROWS, 1, 6 * d)

    bf = lambda w: w.astype(BF16)
    w_att_b, w_four_b, w_pool_b, w_out_b = map(bf, (w_att_br, w_four_br, w_pool_br, w_out))
    w_grp_b, w_g_b, w_u_b, w_d_b = map(bf, (w_pool_grp, w_ffn_gate, w_ffn_up, w_ffn_down))
    w_in_b = bf(w_in)
    w_kq_b = jnp.concatenate([_rope_layout(w_in_b[..., OFF_K:OFF_V]),
                              _rope_layout(w_in_b[..., OFF_Q:OFF_F])], axis=-1)
    vec3 = lambda v: v.reshape(depth, 1, v.shape[-1])
    subln3, pscale3 = vec3(subln_g), vec3(pool_scale)
    ln1_g3, ln1_b3, ln2_g3, ln2_b3 = map(vec3, (ln1_g, ln1_b, ln2_g, ln2_b))

    rope = _rope_tables(seq)
    cs_lat, wpos_lat = _dft_tables(seq)
    cs_ctx, wpos_ctx = _dft_tables(ctx_len)

    n_lat_rows = n_batch * seq
    ctx_blk0 = n_lat_rows // ctx_len

    n_lat_tiles = n_lat_rows // TM_IN
    for l in range(depth):
        last = l == depth - 1
        lam_init = 0.8 - 0.6 * math.exp(-0.3 * l)
        proj = functools.partial(_in_projection, xs, mod, w_in_b, w_kq_b, rope, l, n_lat_tiles,
                                 seq // TM_IN, n_batch)
        if last:
            p = proj(n_row_tiles=n_lat_tiles)
            p = proj(row_tile0=n_lat_tiles, n_row_tiles=xs.shape[0] // TM_IN - n_lat_tiles,
                     n_col_tiles=OFF_Q // TN_IN, prev=p)
        else:
            p = proj()
        att = _attention(p, lam_qk, subln3, l, lam_init, n_batch, seq, ctx_len,
                         with_context=not last)
        four = _fourier(p, cs_lat, wpos_lat, seq, 0, n_batch)
        pool = _pool(p, w_grp_b, pscale3, l, seq, 0, n_batch)
        if not last:
            four = _fourier(p, cs_ctx, wpos_ctx, ctx_len, ctx_blk0, n_batch, prev=four)
            pool = _pool(p, w_grp_b, pscale3, l, ctx_len, ctx_blk0, n_batch, prev=pool)
        xs = _merge(att, four, pool, p, xs, mod, ln1_g3, ln1_b3, w_att_b, w_four_b, w_pool_b,
                    w_out_b, l, alpha, seq // TM_MIX, n_batch,
                    rows=n_lat_rows if last else xs.shape[0])
        xs = _ffn(xs, mod, ln2_g3, ln2_b3, w_g_b, w_u_b, w_d_b, l, alpha, seq // TM_FFN, n_batch)

    return xs.reshape(n_batch, seq, d)
```

```python
import functools
import math

import jax
import jax.numpy as jnp
import numpy as np
from jax import lax
from jax.experimental import pallas as pl
from jax.experimental.pallas import tpu as pltpu

F32 = jnp.float32
BF16 = jnp.bfloat16

D_MODEL = 1024
GRID_W = 64
ATT_HEADS = 8
QK_DIM = 64
V_DIM = 128
ROPE_FREQS = 16
ROPE_BASE = 10000.0
FOURIER_GROUPS = 4
GROUP_C = 128
POOL_WINDOWS = (2, 4, 8, 16)
OFF_K, OFF_V, OFF_Q, OFF_F, OFF_P, OFF_G = 0, 1024, 2048, 3072, 3584, 4096
N_IN = 7168
LN_EPS = 1e-5
LOG2_E = math.log2(math.e)

LANES = 128
SUBLANES = 8
ONES_ROWS = 16
VMEM_LIMIT = 56 * 1024 * 1024

TM_IN = 1024
TN_IN = 1024
TC_IN = 256
ATT_TQ = 512
ATT_QC = 256
ATT_KB = 256
TM_MIX = 512
TM_FFN = 512
TF_FFN = 256
TR_FOUR = 512
POOL_T = 256
POOL_PAD = 128
MOD_ROWS = 16


def _cparams(sem):
    return pltpu.CompilerParams(dimension_semantics=sem, vmem_limit_bytes=VMEM_LIMIT)


def _layer_norm(x):
    mu = jnp.mean(x, axis=-1, keepdims=True)
    xc = x - mu
    var = jnp.mean(xc * xc, axis=-1, keepdims=True)
    return xc * lax.rsqrt(var + LN_EPS)


def _sigmoid(x):
    return 1.0 / (1.0 + jnp.exp(-x))


def _mod_kernel(c_ref, w_ref, b_ref, o_ref):
    c = c_ref[...]
    sc = (c * _sigmoid(c)).astype(BF16)
    o_ref[...] = jnp.dot(sc, w_ref[...].astype(BF16), preferred_element_type=F32) + b_ref[...]


def _mod_vectors(cond, w_mod, b_mod):
    depth, d, n = w_mod.shape
    tn = 1024
    return pl.pallas_call(
        _mod_kernel,
        grid=(depth, n // tn),
        in_specs=[
            pl.BlockSpec((MOD_ROWS, d), lambda l, j: (0, 0)),
            pl.BlockSpec((None, d, tn), lambda l, j: (l, 0, j)),
            pl.BlockSpec((None, 1, tn), lambda l, j: (l, 0, j)),
        ],
        out_specs=pl.BlockSpec((None, MOD_ROWS, tn), lambda l, j: (l, 0, j)),
        out_shape=jax.ShapeDtypeStruct((depth, MOD_ROWS, n), F32),
        compiler_params=_cparams(("arbitrary", "arbitrary")),
        name="mod_vectors",
    )(cond, w_mod, b_mod.reshape(depth, 1, n))


def _inproj_kernel(*refs, aliased):
    if aliased:
        x_ref, sh_ref, sc_ref, w_ref, wkq_ref, cos_ref, sin_ref, _, o_ref, u_ref = refs
    else:
        x_ref, sh_ref, sc_ref, w_ref, wkq_ref, cos_ref, sin_ref, o_ref, u_ref = refs
    j = pl.program_id(1)

    @pl.when(j == 0)
    def _():
        h = _layer_norm(x_ref[...])
        u_ref[...] = (h * (1.0 + sc_ref[...]) + sh_ref[...]).astype(BF16)

    is_k = j == OFF_K // TN_IN
    is_q = j == OFF_Q // TN_IN
    is_gate = j >= OFF_G // TN_IN

    def project(epilogue, weights):
        for c in range(TN_IN // TC_IN):
            cols = slice(c * TC_IN, (c + 1) * TC_IN)
            acc = jnp.dot(u_ref[...], weights[:, cols], preferred_element_type=F32)
            epilogue(acc, cols)

    def rope(acc, cols):
        scale = jnp.where(is_q, QK_DIM ** -0.5 * LOG2_E, 1.0).astype(F32)
        cos, sin = cos_ref[...], sin_ref[...]
        for v in range(TC_IN // LANES):
            xs = acc[:, v * LANES:(v + 1) * LANES]
            r = xs * cos + pltpu.roll(xs, LANES // 2, 1) * sin
            lo = cols.start + v * LANES
            o_ref[:, lo:lo + LANES] = (r * scale).astype(BF16)

    def gate(acc, cols):
        o_ref[:, cols] = _sigmoid(acc).astype(BF16)

    def plain(acc, cols):
        o_ref[:, cols] = acc.astype(BF16)

    pl.when(is_k | is_q)(lambda: project(rope, wkq_ref))
    pl.when(is_gate)(lambda: project(gate, w_ref))
    pl.when(jnp.logical_not(is_k | is_q | is_gate))(lambda: project(plain, w_ref))


def _in_projection(x, mod, w_in, w_kq, rope, layer, n_lat_tiles, tiles_per_batch, n_batch,
                   row_tile0=0, n_row_tiles=None, n_col_tiles=None, prev=None):
    rows, d = x.shape
    cos, sin = rope
    n_rope_blocks = cos.shape[0] // TM_IN - 1
    n_row_tiles = rows // TM_IN if n_row_tiles is None else n_row_tiles
    n_col_tiles = N_IN // TN_IN if n_col_tiles is None else n_col_tiles

    def mod_idx(i):
        return layer * MOD_ROWS + jnp.minimum((i + row_tile0) // tiles_per_batch, n_batch)

    def rope_idx(i, j):
        it = i + row_tile0
        return (jnp.where(it < n_lat_tiles, it % tiles_per_batch, n_rope_blocks), 0)

    in_specs = [
        pl.BlockSpec((TM_IN, d), lambda i, j: (i + row_tile0, 0)),
        pl.BlockSpec((None, 1, d), lambda i, j: (mod_idx(i), 0, 0)),
        pl.BlockSpec((None, 1, d), lambda i, j: (mod_idx(i), 0, 1)),
        pl.BlockSpec((None, d, TN_IN), lambda i, j: (layer, 0, j)),
        pl.BlockSpec((None, d, TN_IN),
                     lambda i, j: (layer, 0, jnp.where(j >= OFF_Q // TN_IN, 1, 0))),
        pl.BlockSpec((TM_IN, LANES), rope_idx),
        pl.BlockSpec((TM_IN, LANES), rope_idx),
    ]
    args = [x, mod, mod, w_in, w_kq, cos, sin]
    aliases = {}
    if prev is not None:
        in_specs.append(pl.BlockSpec(memory_space=pl.ANY))
        args.append(prev)
        aliases = {len(args) - 1: 0}
    return pl.pallas_call(
        functools.partial(_inproj_kernel, aliased=prev is not None),
        grid=(n_row_tiles, n_col_tiles),
        in_specs=in_specs,
        out_specs=pl.BlockSpec((TM_IN, TN_IN), lambda i, j: (i + row_tile0, j)),
        out_shape=jax.ShapeDtypeStruct((rows, N_IN), BF16),
        scratch_shapes=[pltpu.VMEM((TM_IN, d), BF16)],
        input_output_aliases=aliases,
        compiler_params=_cparams(("arbitrary", "arbitrary")),
        name="in_projection",
    )(*args)


_NT = (((1,), (1,)), ((), ()))


def _lam_value(lam_ref, lam_init):
    lq = lam_ref[...]
    return (jnp.exp(jnp.sum(lq[0:1] * lq[1:2], axis=1, keepdims=True))
            - jnp.exp(jnp.sum(lq[2:3] * lq[3:4], axis=1, keepdims=True)) + lam_init)


def _split_maps(q):
    lane = lax.broadcasted_iota(jnp.int32, q.shape, 1)
    first = (lane // (QK_DIM // 2)) % 2 == 0
    zero = jnp.zeros_like(q)
    return jnp.where(first, q, zero), jnp.where(first, zero, q)


def _head_norm(o, g_ref, lam_init):
    y = o * lax.rsqrt(jnp.mean(o * o, axis=-1, keepdims=True) + LN_EPS)
    return (y * g_ref[...] * (1.0 - lam_init)).astype(BF16)


def _attn_latent_kernel(lam_ref, g_ref, q_ref, kl_ref, kc_ref, vl_ref, vc_ref, o_ref,
                        s_even, m_even, s_odd, m_odd, vt_ref, *, lam_init, seq, ctx_len, n_q):
    t = pl.program_id(0)
    lam = _lam_value(lam_ref, lam_init)
    q1, q2 = _split_maps(q_ref[...])
    tq = q_ref.shape[0]

    @pl.when(t == 0)
    def _():
        s_odd[...] = jnp.zeros_like(s_odd)
        m_odd[...] = jnp.zeros_like(m_odd)
        vt_ref[V_DIM:V_DIM + ONES_ROWS, :] = jnp.ones((ONES_ROWS, seq + ctx_len), BF16)

    @pl.when(jnp.maximum(t - 1, 0) % n_q == 0)
    def _():
        vt_ref[0:V_DIM, 0:seq] = vl_ref[...].astype(F32).T.astype(BF16)
        vt_ref[0:V_DIM, seq:seq + ctx_len] = vc_ref[...].astype(F32).T.astype(BF16)

    def step(s_w, m_w, s_r, m_r):
        n_lat_kb = seq // ATT_KB
        for c in range(tq // ATT_QC):
            qs = slice(c * ATT_QC, (c + 1) * ATT_QC)
            qc = (q1[qs, :], q2[qs, :])
            m_old = (m_r[0, 0:1, qs], m_r[1, 0:1, qs])
            m_new = [None, None]
            r = [None, None]
            for kb in range(n_lat_kb + ctx_len // ATT_KB):
                ks = slice(kb * ATT_KB, (kb + 1) * ATT_KB)
                if kb < n_lat_kb:
                    k_blk = kl_ref[ks, :]
                else:
                    k_blk = kc_ref[(kb - n_lat_kb) * ATT_KB:(kb - n_lat_kb + 1) * ATT_KB, :]
                for mp in range(2):
                    sb = lax.dot_general(k_blk, qc[mp], _NT, preferred_element_type=F32)
                    s_w[mp, ks, qs] = sb
                    mb = jnp.max(sb, axis=0, keepdims=True)
                    m_new[mp] = mb if m_new[mp] is None else jnp.maximum(m_new[mp], mb)
                    e = jnp.exp2(s_r[mp, ks, qs] - m_old[mp]).astype(BF16)
                    part = jnp.dot(vt_ref[:, ks], e, preferred_element_type=F32)
                    r[mp] = part if r[mp] is None else r[mp] + part
            for mp in range(2):
                m_w[mp, :, qs] = jnp.broadcast_to(m_new[mp], (SUBLANES, ATT_QC))
            r1, r2 = r
            ot = (r1[0:V_DIM] / r1[V_DIM:V_DIM + 1]
                  - lam * (r2[0:V_DIM] / r2[V_DIM:V_DIM + 1]))
            yt = ot * lax.rsqrt(jnp.mean(ot * ot, axis=0, keepdims=True) + LN_EPS)
            o_ref[qs, :] = (yt.T * g_ref[...] * (1.0 - lam_init)).astype(BF16)

    @pl.when(t % 2 == 0)
    def _():
        step(s_even, m_even, s_odd, m_odd)

    @pl.when(t % 2 == 1)
    def _():
        step(s_odd, m_odd, s_even, m_even)


def _attn_context_kernel(lam_ref, g_ref, q_ref, k_ref, v_ref, _, o_ref, *, lam_init):
    lam = _lam_value(lam_ref, lam_init)
    for h in range(ATT_HEADS):
        cols = slice(h * LANES, (h + 1) * LANES)
        q1, q2 = _split_maps(q_ref[:, cols])
        k = k_ref[:, cols]
        s1 = lax.dot_general(q1, k, _NT, preferred_element_type=F32)
        s2 = lax.dot_general(q2, k, _NT, preferred_element_type=F32)
        e1 = jnp.exp2(s1 - jnp.max(s1, axis=-1, keepdims=True))
        e2 = jnp.exp2(s2 - jnp.max(s2, axis=-1, keepdims=True))
        c1 = 1.0 / jnp.sum(e1, axis=-1, keepdims=True)
        c2 = lam / jnp.sum(e2, axis=-1, keepdims=True)
        a = (e1 * c1 - e2 * c2).astype(BF16)
        o = jnp.dot(a, v_ref[:, cols], preferred_element_type=F32)
        o_ref[:, cols] = _head_norm(o, g_ref, lam_init)


def _attention(p, lam_qk, subln_g, layer, lam_init, n_batch, seq, ctx_len, with_context):
    rows = p.shape[0]
    n_q = seq // ATT_TQ
    n_tiles = n_batch * ATT_HEADS * n_q
    ctx_blk = n_batch * seq // ctx_len
    kcol, vcol, qcol = OFF_K // LANES, OFF_V // LANES, OFF_Q // LANES
    width = ATT_HEADS * V_DIM

    def decode(tile):
        qi = tile % n_q
        bh = tile // n_q
        return bh // ATT_HEADS, bh % ATT_HEADS, qi

    def first(t):
        return decode(jnp.minimum(t, n_tiles - 1))

    def second(t):
        return decode(jnp.maximum(t - 1, 0))

    def q_idx(t):
        b, h, qi = first(t)
        return (b * n_q + qi, qcol + h)

    def o_idx(t):
        b, h, qi = second(t)
        return (b * n_q + qi, h)

    def kv_idx(which, col0, blk0):
        def idx(t):
            b, h, _ = which(t)
            return (blk0 + b, col0 + h)
        return idx

    lam_spec = pl.BlockSpec((None, 4, QK_DIM), lambda *_: (layer, 0, 0))
    g_spec = pl.BlockSpec((None, 1, V_DIM), lambda *_: (layer, 0, 0))
    s_shape = pltpu.VMEM((2, seq + ctx_len, ATT_TQ), F32)
    m_shape = pltpu.VMEM((2, SUBLANES, ATT_TQ), F32)
    att = pl.pallas_call(
        functools.partial(_attn_latent_kernel, lam_init=lam_init, seq=seq, ctx_len=ctx_len,
                          n_q=n_q),
        grid=(n_tiles + 1,),
        in_specs=[
            lam_spec, g_spec,
            pl.BlockSpec((ATT_TQ, LANES), q_idx),
            pl.BlockSpec((seq, LANES), kv_idx(first, kcol, 0)),
            pl.BlockSpec((ctx_len, LANES), kv_idx(first, kcol, ctx_blk)),
            pl.BlockSpec((seq, LANES), kv_idx(second, vcol, 0)),
            pl.BlockSpec((ctx_len, LANES), kv_idx(second, vcol, ctx_blk)),
        ],
        out_specs=pl.BlockSpec((ATT_TQ, LANES), o_idx),
        out_shape=jax.ShapeDtypeStruct((rows, width), BF16),
        scratch_shapes=[s_shape, m_shape, s_shape, m_shape,
                        pltpu.VMEM((V_DIM + ONES_ROWS, seq + ctx_len), BF16)],
        compiler_params=_cparams(("arbitrary",)),
        name="diff_attention",
    )(lam_qk, subln_g, p, p, p, p, p)
    if not with_context:
        return att

    return pl.pallas_call(
        functools.partial(_attn_context_kernel, lam_init=lam_init),
        grid=(n_batch,),
        in_specs=[
            lam_spec, g_spec,
            pl.BlockSpec((ctx_len, width), lambda b: (ctx_blk + b, OFF_Q // width)),
            pl.BlockSpec((ctx_len, width), lambda b: (ctx_blk + b, OFF_K // width)),
            pl.BlockSpec((ctx_len, width), lambda b: (ctx_blk + b, OFF_V // width)),
            pl.BlockSpec(memory_space=pl.ANY),
        ],
        out_specs=pl.BlockSpec((ctx_len, width), lambda b: (ctx_blk + b, 0)),
        out_shape=jax.ShapeDtypeStruct((rows, width), BF16),
        input_output_aliases={5: 0},
        compiler_params=_cparams(("arbitrary",)),
        name="diff_attention_ctx",
    )(lam_qk, subln_g, p, p, p, att)


def _fourier_kernel(*refs, length, norm, aliased):
    if aliased:
        x_ref, cs_ref, w_ref, _, o_ref, xcs_ref = refs
    else:
        x_ref, cs_ref, w_ref, o_ref, xcs_ref = refs

    @pl.when(pl.program_id(1) == 0)
    def _():
        for g in range(FOURIER_GROUPS):
            t = jnp.dot(x_ref[:, g * GROUP_C:(g + 1) * GROUP_C], cs_ref[...],
                        preferred_element_type=F32)
            xcs_ref[0:length, g * GROUP_C:(g + 1) * GROUP_C] = t[:, :GROUP_C].astype(BF16)
            xcs_ref[length:2 * length, g * GROUP_C:(g + 1) * GROUP_C] = t[:, GROUP_C:].astype(BF16)

    y = jnp.dot(w_ref[...], xcs_ref[...], preferred_element_type=F32)
    o_ref[...] = (y * norm).astype(BF16)


def _fourier(p, cs, w_pos, length, row_blk0, n_batch, prev=None):
    rows = p.shape[0]
    width = FOURIER_GROUPS * GROUP_C
    tr = min(TR_FOUR, length)
    n_r = length // tr
    norm = 1.0 / math.sqrt(length * GROUP_C)
    in_specs = [
        pl.BlockSpec((length, width), lambda b, r: (row_blk0 + b, OFF_F // width)),
        pl.BlockSpec((GROUP_C, 2 * GROUP_C), lambda b, r: (0, 0)),
        pl.BlockSpec((tr, 2 * length), lambda b, r: (r, 0)),
    ]
    args = [p, cs, w_pos]
    aliases = {}
    if prev is not None:
        in_specs.append(pl.BlockSpec(memory_space=pl.ANY))
        args.append(prev)
        aliases = {3: 0}
    kern = functools.partial(_fourier_kernel, length=length, norm=norm, aliased=prev is not None)
    return pl.pallas_call(
        kern,
        grid=(n_batch, n_r),
        in_specs=in_specs,
        out_specs=pl.BlockSpec((tr, width), lambda b, r: ((row_blk0 + b) * n_r + r, 0)),
        out_shape=jax.ShapeDtypeStruct((rows, width), BF16),
        scratch_shapes=[pltpu.VMEM((2 * length, width), BF16)],
        input_output_aliases=aliases,
        compiler_params=_cparams(("arbitrary", "arbitrary")),
        name=f"fourier_{length}",
    )(*args)


def _pool_kernel(*refs, length, aliased):
    if aliased:
        x_ref, band_ref, wg_ref, sc_ref, _, o_ref, xp_ref = refs
    else:
        x_ref, band_ref, wg_ref, sc_ref, o_ref, xp_ref = refs
    width = x_ref.shape[1]
    t_rows = min(POOL_T, length)
    slab_rows = t_rows + 2 * POOL_PAD
    xp_ref[0:POOL_PAD, :] = jnp.zeros((POOL_PAD, width), BF16)
    xp_ref[POOL_PAD + length:2 * POOL_PAD + length, :] = jnp.zeros((POOL_PAD, width), BF16)
    xp_ref[POOL_PAD:POOL_PAD + length, :] = x_ref[...]
    for i in range(length // t_rows):
        r0 = i * t_rows
        t = r0 + lax.broadcasted_iota(jnp.int32, (t_rows, 1), 0)
        for g, w in enumerate(POOL_WINDOWS):
            lo = w // 2
            hi = w - lo
            cols = slice(g * GROUP_C, (g + 1) * GROUP_C)
            slab = xp_ref[r0:r0 + slab_rows, cols]
            sums = jnp.dot(band_ref[g], slab, preferred_element_type=F32)
            cnt = (jnp.minimum(t + hi, length) - jnp.maximum(t - lo, 0)).astype(F32)
            d = sums / cnt - x_ref[r0:r0 + t_rows, cols].astype(F32)
            y = jnp.dot(d.astype(BF16), wg_ref[g], preferred_element_type=F32)
            o_ref[r0:r0 + t_rows, cols] = (y * sc_ref[:, cols]).astype(BF16)


def _pool_bands(t_rows):
    tl = np.arange(t_rows)[:, None] + POOL_PAD
    jl = np.arange(t_rows + 2 * POOL_PAD)[None, :]
    bands = []
    for w in POOL_WINDOWS:
        lo = w // 2
        hi = w - lo
        bands.append(((jl >= tl - lo) & (jl < tl + hi)).astype(np.float32))
    return jnp.asarray(np.stack(bands), dtype=BF16)


def _pool(p, w_grp, scale, layer, length, row_blk0, n_batch, prev=None):
    rows = p.shape[0]
    width = len(POOL_WINDOWS) * GROUP_C
    t_rows = min(POOL_T, length)
    bands = _pool_bands(t_rows)
    in_specs = [
        pl.BlockSpec((length, width), lambda b: (row_blk0 + b, OFF_P // width)),
        pl.BlockSpec(bands.shape, lambda b: (0, 0, 0)),
        pl.BlockSpec((None, len(POOL_WINDOWS), GROUP_C, GROUP_C), lambda b: (layer, 0, 0, 0)),
        pl.BlockSpec((None, 1, width), lambda b: (layer, 0, 0)),
    ]
    args = [p, bands, w_grp, scale]
    aliases = {}
    if prev is not None:
        in_specs.append(pl.BlockSpec(memory_space=pl.ANY))
        args.append(prev)
        aliases = {4: 0}
    kern = functools.partial(_pool_kernel, length=length, aliased=prev is not None)
    return pl.pallas_call(
        kern,
        grid=(n_batch,),
        in_specs=in_specs,
        out_specs=pl.BlockSpec((length, width), lambda b: (row_blk0 + b, 0)),
        out_shape=jax.ShapeDtypeStruct((rows, width), BF16),
        scratch_shapes=[pltpu.VMEM((length + 2 * POOL_PAD, width), BF16)],
        input_output_aliases=aliases,
        compiler_params=_cparams(("arbitrary",)),
        name=f"pool_{length}",
    )(*args)


def _merge_kernel(att_ref, four_ref, pool_ref, g0_ref, g1_ref, g2_ref, x_ref, gm_ref,
                  lng_ref, lnb_ref, wa_ref, wf_ref, wp_ref, wo_ref, o_ref, *, alpha):
    ba = jnp.dot(att_ref[...], wa_ref[...], preferred_element_type=F32)
    bf = jnp.dot(four_ref[...], wf_ref[...], preferred_element_type=F32)
    bp = jnp.dot(pool_ref[...], wp_ref[...], preferred_element_type=F32)
    m = (g0_ref[...].astype(F32) * ba + g1_ref[...].astype(F32) * bf
         + g2_ref[...].astype(F32) * bp)
    y = jnp.dot(m.astype(BF16), wo_ref[...], preferred_element_type=F32)
    z = alpha * x_ref[...] + gm_ref[...] * y
    o_ref[...] = _layer_norm(z) * lng_ref[...] + lnb_ref[...]


def _merge(att, four, pool, p, x, mod, ln_g, ln_b, w_att, w_four, w_pool, w_out, layer,
           alpha, tiles_per_batch, n_batch, rows):
    d = x.shape[1]
    tm = TM_MIX
    gcol = OFF_G // d

    def mod_idx(i):
        return layer * MOD_ROWS + jnp.minimum(i // tiles_per_batch, n_batch)

    def const3(i):
        return (layer, 0, 0)

    return pl.pallas_call(
        functools.partial(_merge_kernel, alpha=alpha),
        grid=(rows // tm,),
        in_specs=[
            pl.BlockSpec((tm, att.shape[1]), lambda i: (i, 0)),
            pl.BlockSpec((tm, four.shape[1]), lambda i: (i, 0)),
            pl.BlockSpec((tm, pool.shape[1]), lambda i: (i, 0)),
            pl.BlockSpec((tm, d), lambda i: (i, gcol)),
            pl.BlockSpec((tm, d), lambda i: (i, gcol + 1)),
            pl.BlockSpec((tm, d), lambda i: (i, gcol + 2)),
            pl.BlockSpec((tm, d), lambda i: (i, 0)),
            pl.BlockSpec((None, 1, d), lambda i: (mod_idx(i), 0, 2)),
            pl.BlockSpec((None, 1, d), const3),
            pl.BlockSpec((None, 1, d), const3),
            pl.BlockSpec((None,) + w_att.shape[1:], const3),
            pl.BlockSpec((None,) + w_four.shape[1:], const3),
            pl.BlockSpec((None,) + w_pool.shape[1:], const3),
            pl.BlockSpec((None,) + w_out.shape[1:], const3),
        ],
        out_specs=pl.BlockSpec((tm, d), lambda i: (i, 0)),
        out_shape=jax.ShapeDtypeStruct((rows, d), F32),
        compiler_params=_cparams(("arbitrary",)),
        name="branch_merge",
    )(att, four, pool, p, p, p, x, mod, ln_g, ln_b, w_att, w_four, w_pool, w_out)


def _ffn_kernel(x_ref, sh_ref, sc_ref, gm_ref, lng_ref, lnb_ref, wg_ref, wu_ref, wd_ref,
                o_ref, *, alpha):
    x = x_ref[...]
    h = (_layer_norm(x) * (1.0 + sc_ref[...]) + sh_ref[...]).astype(BF16)
    d_ff = wg_ref.shape[1]
    acc = None
    for c in range(d_ff // TF_FFN):
        cols = slice(c * TF_FFN, (c + 1) * TF_FFN)
        gate = jnp.dot(h, wg_ref[:, cols], preferred_element_type=F32)
        up = jnp.dot(h, wu_ref[:, cols], preferred_element_type=F32)
        a = (gate * _sigmoid(gate) * up).astype(BF16)
        part = jnp.dot(a, wd_ref[cols, :], preferred_element_type=F32)
        acc = part if acc is None else acc + part
    z = alpha * x + gm_ref[...] * acc
    o_ref[...] = _layer_norm(z) * lng_ref[...] + lnb_ref[...]


def _ffn(x, mod, ln_g, ln_b, w_gate, w_up, w_down, layer, alpha, tiles_per_batch, n_batch):
    rows, d = x.shape
    d_ff = w_gate.shape[2]
    tm = TM_FFN
    assert d_ff % TF_FFN == 0

    def mod_idx(i):
        return layer * MOD_ROWS + jnp.minimum(i // tiles_per_batch, n_batch)

    def resident(shape):
        return pl.BlockSpec((None,) + shape, lambda i: (layer, 0, 0),
                            pipeline_mode=pl.Buffered(1))

    return pl.pallas_call(
        functools.partial(_ffn_kernel, alpha=alpha),
        grid=(rows // tm,),
        in_specs=[
            pl.BlockSpec((tm, d), lambda i: (i, 0)),
            pl.BlockSpec((None, 1, d), lambda i: (mod_idx(i), 0, 3)),
            pl.BlockSpec((None, 1, d), lambda i: (mod_idx(i), 0, 4)),
            pl.BlockSpec((None, 1, d), lambda i: (mod_idx(i), 0, 5)),
            pl.BlockSpec((None, 1, d), lambda i: (layer, 0, 0)),
            pl.BlockSpec((None, 1, d), lambda i: (layer, 0, 0)),
            resident((d, d_ff)),
            resident((d, d_ff)),
            resident((d_ff, d)),
        ],
        out_specs=pl.BlockSpec((tm, d), lambda i: (i, 0)),
        out_shape=jax.ShapeDtypeStruct((rows, d), F32),
        compiler_params=_cparams(("arbitrary",)),
        name="swiglu",
    )(x, mod, mod, mod, ln_g, ln_b, w_gate, w_up, w_down)


def _rope_layout(w):
    lead = w.shape[:-1]
    w = w.reshape(lead + (ATT_HEADS, 2, 2, 2, ROPE_FREQS))
    nd = len(lead)
    perm = tuple(range(nd)) + (nd, nd + 3, nd + 1, nd + 2, nd + 4)
    return w.transpose(perm).reshape(lead + (ATT_HEADS * 2 * QK_DIM,))


def _rope_tables(seq):
    rows = seq // GRID_W
    row = np.repeat(np.arange(rows), GRID_W).astype(np.float32)
    col = np.tile(np.arange(GRID_W), rows).astype(np.float32)
    inv = (ROPE_BASE ** (-np.arange(ROPE_FREQS, dtype=np.float32) / ROPE_FREQS)).astype(np.float32)
    ar = row[:, None] * inv[None, :]
    ac = col[:, None] * inv[None, :]
    ang = np.concatenate([ar, ac, ar, ac], axis=-1).astype(np.float32)
    cos = np.cos(ang).astype(np.float32)
    sin = np.sin(ang).astype(np.float32)
    cos = np.concatenate([cos, cos], axis=-1)
    sin = np.concatenate([-sin, sin], axis=-1)

    def with_identity(t, fill):
        ident = np.full((TM_IN, LANES), fill, np.float32)
        return jnp.asarray(np.concatenate([t, ident], axis=0), dtype=F32)

    return with_identity(cos, 1.0), with_identity(sin, 0.0)


def _dft_tables(length):
    def cos_sin(n):
        k = np.arange(n, dtype=np.int64)
        ang = 2.0 * np.pi * ((k[:, None] * k[None, :]) % n).astype(np.float64) / n
        return np.cos(ang), np.sin(ang)

    cc, sc = cos_sin(GROUP_C)
    cl, sl = cos_sin(length)
    cs = jnp.asarray(np.concatenate([cc, sc], axis=1), dtype=F32).astype(BF16)
    w_pos = jnp.asarray(np.concatenate([cl, -sl], axis=1), dtype=F32).astype(BF16)
    return cs, w_pos


def kernel(x, c, ctx, c_ctx, w_mod, b_mod, w_in, lam_qk, subln_g, w_att_br, w_four_br,
           w_pool_grp, pool_scale, w_pool_br, w_out, ln1_g, ln1_b, w_ffn_gate, w_ffn_up,
           w_ffn_down, ln2_g, ln2_b):
    n_batch, seq, d = x.shape
    ctx_len = ctx.shape[1]
    depth = w_mod.shape[0]
    assert d == D_MODEL and seq % ATT_TQ == 0 and seq % TM_IN == 0 and n_batch < MOD_ROWS
    assert (n_batch * ctx_len) % TM_IN == 0 and seq % ctx_len == 0
    alpha = (2 * depth) ** 0.25

    xs = jnp.concatenate([x.reshape(n_batch * seq, d), ctx.reshape(n_batch * ctx_len, d)], axis=0)

    cond = jnp.zeros((MOD_ROWS, d), F32).at[:n_batch].set(c).at[n_batch].set(c_ctx)
    mod = _mod_vectors(cond, w_mod, b_mod).reshape(depth * MOD_ROWS, 1, 6 * d)

    bf = lambda w: w.astype(BF16)
    w_att_b, w_four_b, w_pool_b, w_out_b = map(bf, (w_att_br, w_four_br, w_pool_br, w_out))
    w_grp_b, w_g_b, w_u_b, w_d_b = map(bf, (w_pool_grp, w_ffn_gate, w_ffn_up, w_ffn_down))
    w_in_b = bf(w_in)
    w_kq_b = jnp.concatenate([_rope_layout(w_in_b[..., OFF_K:OFF_V]),
                              _rope_layout(w_in_b[..., OFF_Q:OFF_F])], axis=-1)
    vec3 = lambda v: v.reshape(depth, 1, v.shape[-1])
    subln3, pscale3 = vec3(subln_g), vec3(pool_scale)
    ln1_g3, ln1_b3, ln2_g3, ln2_b3 = map(vec3, (ln1_g, ln1_b, ln2_g, ln2_b))

    rope = _rope_tables(seq)
    cs_lat, wpos_lat = _dft_tables(seq)
    cs_ctx, wpos_ctx = _dft_tables(ctx_len)

    n_lat_rows = n_batch * seq
    ctx_blk0 = n_lat_rows // ctx_len

    n_lat_tiles = n_lat_rows // TM_IN
    for l in range(depth):
        last = l == depth - 1
        lam_init = 0.8 - 0.6 * math.exp(-0.3 * l)
        proj = functools.partial(_in_projection, xs, mod, w_in_b, w_kq_b, rope, l, n_lat_tiles,
                                 seq // TM_IN, n_batch)
        if last:
            p = proj(n_row_tiles=n_lat_tiles)
            p = proj(row_tile0=n_lat_tiles, n_row_tiles=xs.shape[0] // TM_IN - n_lat_tiles,
                     n_col_tiles=OFF_Q // TN_IN, prev=p)
        else:
            p = proj()
        att = _attention(p, lam_qk, subln3, l, lam_init, n_batch, seq, ctx_len,
                         with_context=not last)
        four = _fourier(p, cs_lat, wpos_lat, seq, 0, n_batch)
        pool = _pool(p, w_grp_b, pscale3, l, seq, 0, n_batch)
        if not last:
            four = _fourier(p, cs_ctx, wpos_ctx, ctx_len, ctx_blk0, n_batch, prev=four)
            pool = _pool(p, w_grp_b, pscale3, l, ctx_len, ctx_blk0, n_batch, prev=pool)
        xs = _merge(att, four, pool, p, xs, mod, ln1_g3, ln1_b3, w_att_b, w_four_b, w_pool_b,
                    w_out_b, l, alpha, seq // TM_MIX, n_batch,
                    rows=n_lat_rows if last else xs.shape[0])
        xs = _ffn(xs, mod, ln2_g3, ln2_b3, w_g_b, w_u_b, w_d_b, l, alpha, seq // TM_FFN, n_batch)

    return xs.reshape(n_batch, seq, d)
```

```python
import functools
import math

import jax
import jax.numpy as jnp
import numpy as np
from jax import lax
from jax.experimental import pallas as pl
from jax.experimental.pallas import tpu as pltpu

F32 = jnp.float32
BF16 = jnp.bfloat16

D_MODEL = 1024
GRID_W = 64
ATT_HEADS = 8
QK_DIM = 64
V_DIM = 128
ROPE_FREQS = 16
ROPE_BASE = 10000.0
FOURIER_GROUPS = 4
GROUP_C = 128
POOL_WINDOWS = (2, 4, 8, 16)
OFF_K, OFF_V, OFF_Q, OFF_F, OFF_P, OFF_G = 0, 1024, 2048, 3072, 3584, 4096
N_IN = 7168
LN_EPS = 1e-5
LOG2_E = math.log2(math.e)

LANES = 128
SUBLANES = 8
ONES_ROWS = 16
VMEM_LIMIT = 56 * 1024 * 1024

TM_IN = 1024
TN_IN = 1024
TC_IN = 256
ATT_TQ = 1024
ATT_QC = 256
ATT_KB = 256
TM_MIX = 512
TM_FFN = 512
TF_FFN = 256
TR_FOUR = 512
POOL_T = 256
POOL_PAD = 128
MOD_ROWS = 16


def _cparams(sem):
    return pltpu.CompilerParams(dimension_semantics=sem, vmem_limit_bytes=VMEM_LIMIT)


def _layer_norm(x):
    mu = jnp.mean(x, axis=-1, keepdims=True)
    xc = x - mu
    var = jnp.mean(xc * xc, axis=-1, keepdims=True)
    return xc * lax.rsqrt(var + LN_EPS)


def _sigmoid(x):
    return 0.5 * jnp.tanh(0.5 * x) + 0.5


def _mod_kernel(c_ref, w_ref, b_ref, o_ref):
    c = c_ref[...]
    sc = (c * _sigmoid(c)).astype(BF16)
    o_ref[...] = jnp.dot(sc, w_ref[...].astype(BF16), preferred_element_type=F32) + b_ref[...]


def _mod_vectors(cond, w_mod, b_mod):
    depth, d, n = w_mod.shape
    tn = 1024
    return pl.pallas_call(
        _mod_kernel,
        grid=(depth, n // tn),
        in_specs=[
            pl.BlockSpec((MOD_ROWS, d), lambda l, j: (0, 0)),
            pl.BlockSpec((None, d, tn), lambda l, j: (l, 0, j)),
            pl.BlockSpec((None, 1, tn), lambda l, j: (l, 0, j)),
        ],
        out_specs=pl.BlockSpec((None, MOD_ROWS, tn), lambda l, j: (l, 0, j)),
        out_shape=jax.ShapeDtypeStruct((depth, MOD_ROWS, n), F32),
        compiler_params=_cparams(("arbitrary", "arbitrary")),
        name="mod_vectors",
    )(cond, w_mod, b_mod.reshape(depth, 1, n))


def _inproj_kernel(*refs, aliased):
    if aliased:
        x_ref, sh_ref, sc_ref, w_ref, wkq_ref, cos_ref, sin_ref, _, _, o_ref, vt_ref, u_ref = refs
    else:
        x_ref, sh_ref, sc_ref, w_ref, wkq_ref, cos_ref, sin_ref, o_ref, vt_ref, u_ref = refs
    j = pl.program_id(1)

    @pl.when(j == 0)
    def _():
        h = _layer_norm(x_ref[...])
        u_ref[...] = (h * (1.0 + sc_ref[...]) + sh_ref[...]).astype(BF16)

    is_k = j == OFF_K // TN_IN
    is_q = j == OFF_Q // TN_IN
    is_gate = j >= OFF_G // TN_IN

    def project(epilogue, weights):
        for c in range(TN_IN // TC_IN):
            cols = slice(c * TC_IN, (c + 1) * TC_IN)
            acc = jnp.dot(u_ref[...], weights[:, cols], preferred_element_type=F32)
            epilogue(acc, cols)

    def rope(acc, cols):
        scale = jnp.where(is_q, QK_DIM ** -0.5 * LOG2_E, 1.0).astype(F32)
        cos, sin = cos_ref[...], sin_ref[...]
        for v in range(TC_IN // LANES):
            xs = acc[:, v * LANES:(v + 1) * LANES]
            r = xs * cos + pltpu.roll(xs, LANES // 2, 1) * sin
            lo = cols.start + v * LANES
            o_ref[:, lo:lo + LANES] = (r * scale).astype(BF16)

    def gate(acc, cols):
        o_ref[:, cols] = _sigmoid(acc).astype(BF16)

    def plain(acc, cols):
        o_ref[:, cols] = acc.astype(BF16)

    def values(acc, cols):
        o_ref[:, cols] = acc.astype(BF16)
        vt_ref[cols, :] = acc.T.astype(BF16)

    is_v = j == OFF_V // TN_IN
    pl.when(is_k | is_q)(lambda: project(rope, wkq_ref))
    pl.when(is_gate)(lambda: project(gate, w_ref))
    pl.when(is_v)(lambda: project(values, w_ref))
    pl.when(jnp.logical_not(is_k | is_q | is_gate | is_v))(lambda: project(plain, w_ref))


def _in_projection(x, mod, w_in, w_kq, rope, layer, n_lat_tiles, tiles_per_batch, n_batch,
                   row_tile0=0, n_row_tiles=None, n_col_tiles=None, prev=None):
    rows, d = x.shape
    cos, sin = rope
    n_rope_blocks = cos.shape[0] // TM_IN - 1
    n_row_tiles = rows // TM_IN if n_row_tiles is None else n_row_tiles
    n_col_tiles = N_IN // TN_IN if n_col_tiles is None else n_col_tiles

    def mod_idx(i):
        return layer * MOD_ROWS + jnp.minimum((i + row_tile0) // tiles_per_batch, n_batch)

    def rope_idx(i, j):
        it = i + row_tile0
        return (jnp.where(it < n_lat_tiles, it % tiles_per_batch, n_rope_blocks), 0)

    in_specs = [
        pl.BlockSpec((TM_IN, d), lambda i, j: (i + row_tile0, 0)),
        pl.BlockSpec((None, 1, d), lambda i, j: (mod_idx(i), 0, 0)),
        pl.BlockSpec((None, 1, d), lambda i, j: (mod_idx(i), 0, 1)),
        pl.BlockSpec((None, d, TN_IN), lambda i, j: (layer, 0, j)),
        pl.BlockSpec((None, d, TN_IN),
                     lambda i, j: (layer, 0, jnp.where(j >= OFF_Q // TN_IN, 1, 0))),
        pl.BlockSpec((TM_IN, LANES), rope_idx),
        pl.BlockSpec((TM_IN, LANES), rope_idx),
    ]
    args = [x, mod, mod, w_in, w_kq, cos, sin]
    aliases = {}
    if prev is not None:
        in_specs += [pl.BlockSpec(memory_space=pl.ANY)] * 2
        args += list(prev)
        aliases = {len(args) - 2: 0, len(args) - 1: 1}
    v_width = OFF_Q - OFF_V
    return pl.pallas_call(
        functools.partial(_inproj_kernel, aliased=prev is not None),
        grid=(n_row_tiles, n_col_tiles),
        in_specs=in_specs,
        out_specs=[pl.BlockSpec((TM_IN, TN_IN), lambda i, j: (i + row_tile0, j)),
                   pl.BlockSpec((v_width, TM_IN), lambda i, j: (0, i + row_tile0))],
        out_shape=[jax.ShapeDtypeStruct((rows, N_IN), BF16),
                   jax.ShapeDtypeStruct((v_width, rows), BF16)],
        scratch_shapes=[pltpu.VMEM((TM_IN, d), BF16)],
        input_output_aliases=aliases,
        compiler_params=_cparams(("arbitrary", "arbitrary")),
        name="in_projection",
    )(*args)


_NT = (((1,), (1,)), ((), ()))


def _lam_value(lam_ref, lam_init):
    lq = lam_ref[...]
    return (jnp.exp(jnp.sum(lq[0:1] * lq[1:2], axis=1, keepdims=True))
            - jnp.exp(jnp.sum(lq[2:3] * lq[3:4], axis=1, keepdims=True)) + lam_init)


def _split_maps(q):
    lane = lax.broadcasted_iota(jnp.int32, q.shape, 1)
    first = (lane // (QK_DIM // 2)) % 2 == 0
    zero = jnp.zeros_like(q)
    return jnp.where(first, q, zero), jnp.where(first, zero, q)


def _head_norm(o, g_ref, lam_init):
    y = o * lax.rsqrt(jnp.mean(o * o, axis=-1, keepdims=True) + LN_EPS)
    return (y * g_ref[...] * (1.0 - lam_init)).astype(BF16)


def _attn_latent_kernel(lam_ref, g_ref, q_ref, kl_ref, kc_ref, vtl_ref, vtc_ref, o_ref,
                        s_even, m_even, s_odd, m_odd, *, lam_init, seq, ctx_len):
    t = pl.program_id(0)
    lam = _lam_value(lam_ref, lam_init)
    q1, q2 = _split_maps(q_ref[...])
    tq = q_ref.shape[0]

    @pl.when(t == 0)
    def _():
        s_odd[...] = jnp.zeros_like(s_odd)
        m_odd[...] = jnp.zeros_like(m_odd)

    ones_rows = jnp.ones((ONES_ROWS, ATT_KB), BF16)

    def step(s_w, m_w, s_r, m_r):
        n_lat_kb = seq // ATT_KB
        for c in range(tq // ATT_QC):
            qs = slice(c * ATT_QC, (c + 1) * ATT_QC)
            qc = (q1[qs, :], q2[qs, :])
            m_old = (m_r[0, 0:1, qs], m_r[1, 0:1, qs])
            m_new = [None, None]
            r = [None, None]
            for kb in range(n_lat_kb + ctx_len // ATT_KB):
                ks = slice(kb * ATT_KB, (kb + 1) * ATT_KB)
                if kb < n_lat_kb:
                    k_blk, vt_blk = kl_ref[ks, :], vtl_ref[:, ks]
                else:
                    cs = slice((kb - n_lat_kb) * ATT_KB, (kb - n_lat_kb + 1) * ATT_KB)
                    k_blk, vt_blk = kc_ref[cs, :], vtc_ref[:, cs]
                vt_ones = jnp.concatenate([vt_blk, ones_rows], axis=0)
                for mp in range(2):
                    sb = lax.dot_general(k_blk, qc[mp], _NT, preferred_element_type=F32)
                    s_w[mp, ks, qs] = sb
                    mb = jnp.max(sb, axis=0, keepdims=True)
                    m_new[mp] = mb if m_new[mp] is None else jnp.maximum(m_new[mp], mb)
                    e = jnp.exp2(s_r[mp, ks, qs] - m_old[mp]).astype(BF16)
                    part = jnp.dot(vt_ones, e, preferred_element_type=F32)
                    r[mp] = part if r[mp] is None else r[mp] + part
            for mp in range(2):
                m_w[mp, :, qs] = jnp.broadcast_to(m_new[mp], (SUBLANES, ATT_QC))
            r1, r2 = r
            ot = (r1[0:V_DIM] / r1[V_DIM:V_DIM + 1]
                  - lam * (r2[0:V_DIM] / r2[V_DIM:V_DIM + 1]))
            yt = ot * lax.rsqrt(jnp.mean(ot * ot, axis=0, keepdims=True) + LN_EPS)
            o_ref[qs, :] = (yt.T * g_ref[...] * (1.0 - lam_init)).astype(BF16)

    @pl.when(t % 2 == 0)
    def _():
        step(s_even, m_even, s_odd, m_odd)

    @pl.when(t % 2 == 1)
    def _():
        step(s_odd, m_odd, s_even, m_even)


def _attn_context_kernel(lam_ref, g_ref, q_ref, k_ref, v_ref, _, o_ref, *, lam_init):
    lam = _lam_value(lam_ref, lam_init)
    for h in range(ATT_HEADS):
        cols = slice(h * LANES, (h + 1) * LANES)
        q1, q2 = _split_maps(q_ref[:, cols])
        k = k_ref[:, cols]
        s1 = lax.dot_general(q1, k, _NT, preferred_element_type=F32)
        s2 = lax.dot_general(q2, k, _NT, preferred_element_type=F32)
        e1 = jnp.exp2(s1 - jnp.max(s1, axis=-1, keepdims=True))
        e2 = jnp.exp2(s2 - jnp.max(s2, axis=-1, keepdims=True))
        c1 = 1.0 / jnp.sum(e1, axis=-1, keepdims=True)
        c2 = lam / jnp.sum(e2, axis=-1, keepdims=True)
        a = (e1 * c1 - e2 * c2).astype(BF16)
        o = jnp.dot(a, v_ref[:, cols], preferred_element_type=F32)
        o_ref[:, cols] = _head_norm(o, g_ref, lam_init)


def _attention(p, vt, lam_qk, subln_g, layer, lam_init, n_batch, seq, ctx_len, with_context):
    rows = p.shape[0]
    n_q = seq // ATT_TQ
    n_tiles = n_batch * ATT_HEADS * n_q
    ctx_blk = n_batch * seq // ctx_len
    kcol, vcol, qcol = OFF_K // LANES, OFF_V // LANES, OFF_Q // LANES
    width = ATT_HEADS * V_DIM

    def decode(tile):
        qi = tile % n_q
        bh = tile // n_q
        return bh // ATT_HEADS, bh % ATT_HEADS, qi

    def first(t):
        return decode(jnp.minimum(t, n_tiles - 1))

    def second(t):
        return decode(jnp.maximum(t - 1, 0))

    def q_idx(t):
        b, h, qi = first(t)
        return (b * n_q + qi, qcol + h)

    def o_idx(t):
        b, h, qi = second(t)
        return (b * n_q + qi, h)

    def kv_idx(which, col0, blk0):
        def idx(t):
            b, h, _ = which(t)
            return (blk0 + b, col0 + h)
        return idx

    lam_spec = pl.BlockSpec((None, 4, QK_DIM), lambda *_: (layer, 0, 0))
    g_spec = pl.BlockSpec((None, 1, V_DIM), lambda *_: (layer, 0, 0))
    s_shape = pltpu.VMEM((2, seq + ctx_len, ATT_TQ), F32)
    m_shape = pltpu.VMEM((2, SUBLANES, ATT_TQ), F32)
    def vt_idx(blk0):
        def idx(t):
            b, h, _ = second(t)
            return (h, blk0 + b)
        return idx

    att = pl.pallas_call(
        functools.partial(_attn_latent_kernel, lam_init=lam_init, seq=seq, ctx_len=ctx_len),
        grid=(n_tiles + 1,),
        in_specs=[
            lam_spec, g_spec,
            pl.BlockSpec((ATT_TQ, LANES), q_idx),
            pl.BlockSpec((seq, LANES), kv_idx(first, kcol, 0)),
            pl.BlockSpec((ctx_len, LANES), kv_idx(first, kcol, ctx_blk)),
            pl.BlockSpec((V_DIM, seq), vt_idx(0)),
            pl.BlockSpec((V_DIM, ctx_len), vt_idx(ctx_blk)),
        ],
        out_specs=pl.BlockSpec((ATT_TQ, LANES), o_idx),
        out_shape=jax.ShapeDtypeStruct((rows, width), BF16),
        scratch_shapes=[s_shape, m_shape, s_shape, m_shape],
        compiler_params=_cparams(("arbitrary",)),
        name="diff_attention",
    )(lam_qk, subln_g, p, p, p, vt, vt)
    if not with_context:
        return att

    return pl.pallas_call(
        functools.partial(_attn_context_kernel, lam_init=lam_init),
        grid=(n_batch,),
        in_specs=[
            lam_spec, g_spec,
            pl.BlockSpec((ctx_len, width), lambda b: (ctx_blk + b, OFF_Q // width)),
            pl.BlockSpec((ctx_len, width), lambda b: (ctx_blk + b, OFF_K // width)),
            pl.BlockSpec((ctx_len, width), lambda b: (ctx_blk + b, OFF_V // width)),
            pl.BlockSpec(memory_space=pl.ANY),
        ],
        out_specs=pl.BlockSpec((ctx_len, width), lambda b: (ctx_blk + b, 0)),
        out_shape=jax.ShapeDtypeStruct((rows, width), BF16),
        input_output_aliases={5: 0},
        compiler_params=_cparams(("arbitrary",)),
        name="diff_attention_ctx",
    )(lam_qk, subln_g, p, p, p, att)


def _fourier_kernel(*refs, length, norm, aliased):
    if aliased:
        x_ref, cs_ref, w_ref, _, o_ref, xcs_ref = refs
    else:
        x_ref, cs_ref, w_ref, o_ref, xcs_ref = refs

    @pl.when(pl.program_id(1) == 0)
    def _():
        for g in range(FOURIER_GROUPS):
            t = jnp.dot(x_ref[:, g * GROUP_C:(g + 1) * GROUP_C], cs_ref[...],
                        preferred_element_type=F32)
            xcs_ref[0:length, g * GROUP_C:(g + 1) * GROUP_C] = t[:, :GROUP_C].astype(BF16)
            xcs_ref[length:2 * length, g * GROUP_C:(g + 1) * GROUP_C] = t[:, GROUP_C:].astype(BF16)

    y = jnp.dot(w_ref[...], xcs_ref[...], preferred_element_type=F32)
    o_ref[...] = (y * norm).astype(BF16)


def _fourier(p, cs, w_pos, length, row_blk0, n_batch, prev=None):
    rows = p.shape[0]
    width = FOURIER_GROUPS * GROUP_C
    tr = min(TR_FOUR, length)
    n_r = length // tr
    norm = 1.0 / math.sqrt(length * GROUP_C)
    in_specs = [
        pl.BlockSpec((length, width), lambda b, r: (row_blk0 + b, OFF_F // width)),
        pl.BlockSpec((GROUP_C, 2 * GROUP_C), lambda b, r: (0, 0)),
        pl.BlockSpec((tr, 2 * length), lambda b, r: (r, 0)),
    ]
    args = [p, cs, w_pos]
    aliases = {}
    if prev is not None:
        in_specs.append(pl.BlockSpec(memory_space=pl.ANY))
        args.append(prev)
        aliases = {3: 0}
    kern = functools.partial(_fourier_kernel, length=length, norm=norm, aliased=prev is not None)
    return pl.pallas_call(
        kern,
        grid=(n_batch, n_r),
        in_specs=in_specs,
        out_specs=pl.BlockSpec((tr, width), lambda b, r: ((row_blk0 + b) * n_r + r, 0)),
        out_shape=jax.ShapeDtypeStruct((rows, width), BF16),
        scratch_shapes=[pltpu.VMEM((2 * length, width), BF16)],
        input_output_aliases=aliases,
        compiler_params=_cparams(("arbitrary", "arbitrary")),
        name=f"fourier_{length}",
    )(*args)


def _pool_kernel(*refs, length, aliased):
    if aliased:
        x_ref, band_ref, wg_ref, sc_ref, _, o_ref, xp_ref = refs
    else:
        x_ref, band_ref, wg_ref, sc_ref, o_ref, xp_ref = refs
    width = x_ref.shape[1]
    t_rows = min(POOL_T, length)
    slab_rows = t_rows + 2 * POOL_PAD
    xp_ref[0:POOL_PAD, :] = jnp.zeros((POOL_PAD, width), BF16)
    xp_ref[POOL_PAD + length:2 * POOL_PAD + length, :] = jnp.zeros((POOL_PAD, width), BF16)
    xp_ref[POOL_PAD:POOL_PAD + length, :] = x_ref[...]
    for i in range(length // t_rows):
        r0 = i * t_rows
        t = r0 + lax.broadcasted_iota(jnp.int32, (t_rows, 1), 0)
        for g, w in enumerate(POOL_WINDOWS):
            lo = w // 2
            hi = w - lo
            cols = slice(g * GROUP_C, (g + 1) * GROUP_C)
            slab = xp_ref[r0:r0 + slab_rows, cols]
            sums = jnp.dot(band_ref[g], slab, preferred_element_type=F32)
            cnt = (jnp.minimum(t + hi, length) - jnp.maximum(t - lo, 0)).astype(F32)
            d = sums / cnt - x_ref[r0:r0 + t_rows, cols].astype(F32)
            y = jnp.dot(d.astype(BF16), wg_ref[g], preferred_element_type=F32)
            o_ref[r0:r0 + t_rows, cols] = (y * sc_ref[:, cols]).astype(BF16)


def _pool_bands(t_rows):
    tl = np.arange(t_rows)[:, None] + POOL_PAD
    jl = np.arange(t_rows + 2 * POOL_PAD)[None, :]
    bands = []
    for w in POOL_WINDOWS:
        lo = w // 2
        hi = w - lo
        bands.append(((jl >= tl - lo) & (jl < tl + hi)).astype(np.float32))
    return jnp.asarray(np.stack(bands), dtype=BF16)


def _pool(p, w_grp, scale, layer, length, row_blk0, n_batch, prev=None):
    rows = p.shape[0]
    width = len(POOL_WINDOWS) * GROUP_C
    t_rows = min(POOL_T, length)
    bands = _pool_bands(t_rows)
    in_specs = [
        pl.BlockSpec((length, width), lambda b: (row_blk0 + b, OFF_P // width)),
        pl.BlockSpec(bands.shape, lambda b: (0, 0, 0)),
        pl.BlockSpec((None, len(POOL_WINDOWS), GROUP_C, GROUP_C), lambda b: (layer, 0, 0, 0)),
        pl.BlockSpec((None, 1, width), lambda b: (layer, 0, 0)),
    ]
    args = [p, bands, w_grp, scale]
    aliases = {}
    if prev is not None:
        in_specs.append(pl.BlockSpec(memory_space=pl.ANY))
        args.append(prev)
        aliases = {4: 0}
    kern = functools.partial(_pool_kernel, length=length, aliased=prev is not None)
    return pl.pallas_call(
        kern,
        grid=(n_batch,),
        in_specs=in_specs,
        out_specs=pl.BlockSpec((length, width), lambda b: (row_blk0 + b, 0)),
        out_shape=jax.ShapeDtypeStruct((rows, width), BF16),
        scratch_shapes=[pltpu.VMEM((length + 2 * POOL_PAD, width), BF16)],
        input_output_aliases=aliases,
        compiler_params=_cparams(("arbitrary",)),
        name=f"pool_{length}",
    )(*args)


def _merge_kernel(att_ref, four_ref, pool_ref, g0_ref, g1_ref, g2_ref, x_ref, gm_ref,
                  lng_ref, lnb_ref, wa_ref, wf_ref, wp_ref, wo_ref, o_ref, *, alpha):
    ba = jnp.dot(att_ref[...], wa_ref[...], preferred_element_type=F32)
    bf = jnp.dot(four_ref[...], wf_ref[...], preferred_element_type=F32)
    bp = jnp.dot(pool_ref[...], wp_ref[...], preferred_element_type=F32)
    m = (g0_ref[...].astype(F32) * ba + g1_ref[...].astype(F32) * bf
         + g2_ref[...].astype(F32) * bp)
    y = jnp.dot(m.astype(BF16), wo_ref[...], preferred_element_type=F32)
    z = alpha * x_ref[...] + gm_ref[...] * y
    o_ref[...] = _layer_norm(z) * lng_ref[...] + lnb_ref[...]


def _merge(att, four, pool, p, x, mod, ln_g, ln_b, w_att, w_four, w_pool, w_out, layer,
           alpha, tiles_per_batch, n_batch, rows):
    d = x.shape[1]
    tm = TM_MIX
    gcol = OFF_G // d

    def mod_idx(i):
        return layer * MOD_ROWS + jnp.minimum(i // tiles_per_batch, n_batch)

    def const3(i):
        return (layer, 0, 0)

    return pl.pallas_call(
        functools.partial(_merge_kernel, alpha=alpha),
        grid=(rows // tm,),
        in_specs=[
            pl.BlockSpec((tm, att.shape[1]), lambda i: (i, 0)),
            pl.BlockSpec((tm, four.shape[1]), lambda i: (i, 0)),
            pl.BlockSpec((tm, pool.shape[1]), lambda i: (i, 0)),
            pl.BlockSpec((tm, d), lambda i: (i, gcol)),
            pl.BlockSpec((tm, d), lambda i: (i, gcol + 1)),
            pl.BlockSpec((tm, d), lambda i: (i, gcol + 2)),
            pl.BlockSpec((tm, d), lambda i: (i, 0)),
            pl.BlockSpec((None, 1, d), lambda i: (mod_idx(i), 0, 2)),
            pl.BlockSpec((None, 1, d), const3),
            pl.BlockSpec((None, 1, d), const3),
            pl.BlockSpec((None,) + w_att.shape[1:], const3),
            pl.BlockSpec((None,) + w_four.shape[1:], const3),
            pl.BlockSpec((None,) + w_pool.shape[1:], const3),
            pl.BlockSpec((None,) + w_out.shape[1:], const3),
        ],
        out_specs=pl.BlockSpec((tm, d), lambda i: (i, 0)),
        out_shape=jax.ShapeDtypeStruct((rows, d), F32),
        compiler_params=_cparams(("arbitrary",)),
        name="branch_merge",
    )(att, four, pool, p, p, p, x, mod, ln_g, ln_b, w_att, w_four, w_pool, w_out)


def _ffn_kernel(x_ref, sh_ref, sc_ref, gm_ref, lng_ref, lnb_ref, wg_ref, wu_ref, wd_ref,
                o_ref, *, alpha):
    x = x_ref[...]
    h = (_layer_norm(x) * (1.0 + sc_ref[...]) + sh_ref[...]).astype(BF16)
    d_ff = wg_ref.shape[1]
    acc = None
    for c in range(d_ff // TF_FFN):
        cols = slice(c * TF_FFN, (c + 1) * TF_FFN)
        gate = jnp.dot(h, wg_ref[:, cols], preferred_element_type=F32)
        up = jnp.dot(h, wu_ref[:, cols], preferred_element_type=F32)
        a = (gate * _sigmoid(gate) * up).astype(BF16)
        part = jnp.dot(a, wd_ref[cols, :], preferred_element_type=F32)
        acc = part if acc is None else acc + part
    z = alpha * x + gm_ref[...] * acc
    o_ref[...] = _layer_norm(z) * lng_ref[...] + lnb_ref[...]


def _ffn(x, mod, ln_g, ln_b, w_gate, w_up, w_down, layer, alpha, tiles_per_batch, n_batch):
    rows, d = x.shape
    d_ff = w_gate.shape[2]
    tm = TM_FFN
    assert d_ff % TF_FFN == 0

    def mod_idx(i):
        return layer * MOD_ROWS + jnp.minimum(i // tiles_per_batch, n_batch)

    def resident(shape):
        return pl.BlockSpec((None,) + shape, lambda i: (layer, 0, 0),
                            pipeline_mode=pl.Buffered(1))

    return pl.pallas_call(
        functools.partial(_ffn_kernel, alpha=alpha),
        grid=(rows // tm,),
        in_specs=[
            pl.BlockSpec((tm, d), lambda i: (i, 0)),
            pl.BlockSpec((None, 1, d), lambda i: (mod_idx(i), 0, 3)),
            pl.BlockSpec((None, 1, d), lambda i: (mod_idx(i), 0, 4)),
            pl.BlockSpec((None, 1, d), lambda i: (mod_idx(i), 0, 5)),
            pl.BlockSpec((None, 1, d), lambda i: (layer, 0, 0)),
            pl.BlockSpec((None, 1, d), lambda i: (layer, 0, 0)),
            resident((d, d_ff)),
            resident((d, d_ff)),
            resident((d_ff, d)),
        ],
        out_specs=pl.BlockSpec((tm, d), lambda i: (i, 0)),
        out_shape=jax.ShapeDtypeStruct((rows, d), F32),
        compiler_params=_cparams(("arbitrary",)),
        name="swiglu",
    )(x, mod, mod, mod, ln_g, ln_b, w_gate, w_up, w_down)


def _rope_layout(w):
    lead = w.shape[:-1]
    w = w.reshape(lead + (ATT_HEADS, 2, 2, 2, ROPE_FREQS))
    nd = len(lead)
    perm = tuple(range(nd)) + (nd, nd + 3, nd + 1, nd + 2, nd + 4)
    return w.transpose(perm).reshape(lead + (ATT_HEADS * 2 * QK_DIM,))


def _rope_tables(seq):
    rows = seq // GRID_W
    row = np.repeat(np.arange(rows), GRID_W).astype(np.float32)
    col = np.tile(np.arange(GRID_W), rows).astype(np.float32)
    inv = (ROPE_BASE ** (-np.arange(ROPE_FREQS, dtype=np.float32) / ROPE_FREQS)).astype(np.float32)
    ar = row[:, None] * inv[None, :]
    ac = col[:, None] * inv[None, :]
    ang = np.concatenate([ar, ac, ar, ac], axis=-1).astype(np.float32)
    cos = np.cos(ang).astype(np.float32)
    sin = np.sin(ang).astype(np.float32)
    cos = np.concatenate([cos, cos], axis=-1)
    sin = np.concatenate([-sin, sin], axis=-1)

    def with_identity(t, fill):
        ident = np.full((TM_IN, LANES), fill, np.float32)
        return jnp.asarray(np.concatenate([t, ident], axis=0), dtype=F32)

    return with_identity(cos, 1.0), with_identity(sin, 0.0)


def _dft_tables(length):
    def cos_sin(n):
        k = np.arange(n, dtype=np.int64)
        ang = 2.0 * np.pi * ((k[:, None] * k[None, :]) % n).astype(np.float64) / n
        return np.cos(ang), np.sin(ang)

    cc, sc = cos_sin(GROUP_C)
    cl, sl = cos_sin(length)
    cs = jnp.asarray(np.concatenate([cc, sc], axis=1), dtype=F32).astype(BF16)
    w_pos = jnp.asarray(np.concatenate([cl, -sl], axis=1), dtype=F32).astype(BF16)
    return cs, w_pos


def kernel(x, c, ctx, c_ctx, w_mod, b_mod, w_in, lam_qk, subln_g, w_att_br, w_four_br,
           w_pool_grp, pool_scale, w_pool_br, w_out, ln1_g, ln1_b, w_ffn_gate, w_ffn_up,
           w_ffn_down, ln2_g, ln2_b):
    n_batch, seq, d = x.shape
    ctx_len = ctx.shape[1]
    depth = w_mod.shape[0]
    assert d == D_MODEL and seq % ATT_TQ == 0 and seq % TM_IN == 0 and n_batch < MOD_ROWS
    assert (n_batch * ctx_len) % TM_IN == 0 and seq % ctx_len == 0
    alpha = (2 * depth) ** 0.25

    xs = jnp.concatenate([x.reshape(n_batch * seq, d), ctx.reshape(n_batch * ctx_len, d)], axis=0)

    cond = jnp.zeros((MOD_ROWS, d), F32).at[:n_batch].set(c).at[n_batch].set(c_ctx)
    mod = _mod_vectors(cond, w_mod, b_mod).reshape(depth * MOD_ROWS, 1, 6 * d)

    bf = lambda w: w.astype(BF16)
    w_att_b, w_four_b, w_pool_b, w_out_b = map(bf, (w_att_br, w_four_br, w_pool_br, w_out))
    w_grp_b, w_g_b, w_u_b, w_d_b = map(bf, (w_pool_grp, w_ffn_gate, w_ffn_up, w_ffn_down))
    w_in_b = bf(w_in)
    w_kq_b = jnp.concatenate([_rope_layout(w_in_b[..., OFF_K:OFF_V]),
                              _rope_layout(w_in_b[..., OFF_Q:OFF_F])], axis=-1)
    vec3 = lambda v: v.reshape(depth, 1, v.shape[-1])
    subln3, pscale3 = vec3(subln_g), vec3(pool_scale)
    ln1_g3, ln1_b3, ln2_g3, ln2_b3 = map(vec3, (ln1_g, ln1_b, ln2_g, ln2_b))

    rope = _rope_tables(seq)
    cs_lat, wpos_lat = _dft_tables(seq)
    cs_ctx, wpos_ctx = _dft_tables(ctx_len)

    n_lat_rows = n_batch * seq
    ctx_blk0 = n_lat_rows // ctx_len

    n_lat_tiles = n_lat_rows // TM_IN
    for l in range(depth):
        last = l == depth - 1
        lam_init = 0.8 - 0.6 * math.exp(-0.3 * l)
        proj = functools.partial(_in_projection, xs, mod, w_in_b, w_kq_b, rope, l, n_lat_tiles,
                                 seq // TM_IN, n_batch)
        if last:
            pv = proj(n_row_tiles=n_lat_tiles)
            p, vt = proj(row_tile0=n_lat_tiles, n_row_tiles=xs.shape[0] // TM_IN - n_lat_tiles,
                         n_col_tiles=OFF_Q // TN_IN, prev=pv)
        else:
            p, vt = proj()
        att = _attention(p, vt, lam_qk, subln3, l, lam_init, n_batch, seq, ctx_len,
                         with_context=not last)
        four = _fourier(p, cs_lat, wpos_lat, seq, 0, n_batch)
        pool = _pool(p, w_grp_b, pscale3, l, seq, 0, n_batch)
        if not last:
            four = _fourier(p, cs_ctx, wpos_ctx, ctx_len, ctx_blk0, n_batch, prev=four)
            pool = _pool(p, w_grp_b, pscale3, l, ctx_len, ctx_blk0, n_batch, prev=pool)
        xs = _merge(att, four, pool, p, xs, mod, ln1_g3, ln1_b3, w_att_b, w_four_b, w_pool_b,
                    w_out_b, l, alpha, seq // TM_MIX, n_batch,
                    rows=n_lat_rows if last else xs.shape[0])
        xs = _ffn(xs, mod, ln2_g3, ln2_b3, w_g_b, w_u_b, w_d_b, l, alpha, seq // TM_FFN, n_batch)

    return xs.reshape(n_batch, seq, d)
```

```python
import functools
import math

import jax
import jax.numpy as jnp
import numpy as np
from jax import lax
from jax.experimental import pallas as pl
from jax.experimental.pallas import tpu as pltpu

F32 = jnp.float32
BF16 = jnp.bfloat16

D_MODEL = 1024
GRID_W = 64
ATT_HEADS = 8
QK_DIM = 64
V_DIM = 128
ROPE_FREQS = 16
ROPE_BASE = 10000.0
FOURIER_GROUPS = 4
GROUP_C = 128
POOL_WINDOWS = (2, 4, 8, 16)
OFF_K, OFF_V, OFF_Q, OFF_F, OFF_P, OFF_G = 0, 1024, 2048, 3072, 3584, 4096
N_IN = 7168
LN_EPS = 1e-5
LOG2_E = math.log2(math.e)

LANES = 128
SUBLANES = 8
ONES_ROWS = 16
VMEM_LIMIT = 56 * 1024 * 1024

TM_IN = 1024
TN_IN = 1024
TC_IN = 256
ATT_TQ = 1024
ATT_QC = 256
ATT_KB = 256
TM_MIX = 512
TM_FFN = 512
TF_FFN = 256
TR_FOUR = 512
POOL_T = 256
POOL_PAD = 128
MOD_ROWS = 16


def _cparams(sem):
    return pltpu.CompilerParams(dimension_semantics=sem, vmem_limit_bytes=VMEM_LIMIT)


def _layer_norm(x):
    mu = jnp.mean(x, axis=-1, keepdims=True)
    xc = x - mu
    var = jnp.mean(xc * xc, axis=-1, keepdims=True)
    return xc * lax.rsqrt(var + LN_EPS)


def _sigmoid(x):
    return 0.5 * jnp.tanh(0.5 * x) + 0.5


def _mod_kernel(c_ref, w_ref, b_ref, o_ref):
    c = c_ref[...]
    sc = (c * _sigmoid(c)).astype(BF16)
    o_ref[...] = jnp.dot(sc, w_ref[...].astype(BF16), preferred_element_type=F32) + b_ref[...]


def _mod_vectors(cond, w_mod, b_mod):
    depth, d, n = w_mod.shape
    tn = 1024
    return pl.pallas_call(
        _mod_kernel,
        grid=(depth, n // tn),
        in_specs=[
            pl.BlockSpec((MOD_ROWS, d), lambda l, j: (0, 0)),
            pl.BlockSpec((None, d, tn), lambda l, j: (l, 0, j)),
            pl.BlockSpec((None, 1, tn), lambda l, j: (l, 0, j)),
        ],
        out_specs=pl.BlockSpec((None, MOD_ROWS, tn), lambda l, j: (l, 0, j)),
        out_shape=jax.ShapeDtypeStruct((depth, MOD_ROWS, n), F32),
        compiler_params=_cparams(("arbitrary", "arbitrary")),
        name="mod_vectors",
    )(cond, w_mod, b_mod.reshape(depth, 1, n))


def _inproj_kernel(*refs, aliased):
    if aliased:
        x_ref, sh_ref, sc_ref, w_ref, cos_ref, sin_ref, _, _, o_ref, vt_ref, u_ref = refs
    else:
        x_ref, sh_ref, sc_ref, w_ref, cos_ref, sin_ref, o_ref, vt_ref, u_ref = refs
    j = pl.program_id(1)

    @pl.when(j == 0)
    def _():
        h = _layer_norm(x_ref[...])
        u_ref[...] = (h * (1.0 + sc_ref[...]) + sh_ref[...]).astype(BF16)

    is_k = j == OFF_K // TN_IN
    is_q = j == OFF_Q // TN_IN
    is_gate = j >= OFF_G // TN_IN

    def project(epilogue):
        for c in range(TN_IN // TC_IN):
            cols = slice(c * TC_IN, (c + 1) * TC_IN)
            acc = jnp.dot(u_ref[...], w_ref[j, :, cols], preferred_element_type=F32)
            epilogue(acc, cols)

    def rope(acc, cols):
        scale = jnp.where(is_q, QK_DIM ** -0.5 * LOG2_E, 1.0).astype(F32)
        cos, sin = cos_ref[...], sin_ref[...]
        for v in range(TC_IN // LANES):
            xs = acc[:, v * LANES:(v + 1) * LANES]
            r = xs * cos + pltpu.roll(xs, LANES // 2, 1) * sin
            lo = cols.start + v * LANES
            o_ref[:, lo:lo + LANES] = (r * scale).astype(BF16)

    def gate(acc, cols):
        o_ref[:, cols] = _sigmoid(acc).astype(BF16)

    def plain(acc, cols):
        o_ref[:, cols] = acc.astype(BF16)

    def values(acc, cols):
        o_ref[:, cols] = acc.astype(BF16)
        vt_ref[cols, :] = acc.T.astype(BF16)

    is_v = j == OFF_V // TN_IN
    pl.when(is_k | is_q)(lambda: project(rope))
    pl.when(is_gate)(lambda: project(gate))
    pl.when(is_v)(lambda: project(values))
    pl.when(jnp.logical_not(is_k | is_q | is_gate | is_v))(lambda: project(plain))


def _in_projection(x, mod, w_tiles, rope, layer, n_lat_tiles, tiles_per_batch, n_batch,
                   row_tile0=0, n_row_tiles=None, n_col_tiles=None, prev=None):
    rows, d = x.shape
    cos, sin = rope
    n_rope_blocks = cos.shape[0] // TM_IN - 1
    n_row_tiles = rows // TM_IN if n_row_tiles is None else n_row_tiles
    n_col_tiles = N_IN // TN_IN if n_col_tiles is None else n_col_tiles

    def mod_idx(i):
        return layer * MOD_ROWS + jnp.minimum((i + row_tile0) // tiles_per_batch, n_batch)

    def rope_idx(i, j):
        it = i + row_tile0
        return (jnp.where(it < n_lat_tiles, it % tiles_per_batch, n_rope_blocks), 0)

    in_specs = [
        pl.BlockSpec((TM_IN, d), lambda i, j: (i + row_tile0, 0)),
        pl.BlockSpec((None, 1, d), lambda i, j: (mod_idx(i), 0, 0)),
        pl.BlockSpec((None, 1, d), lambda i, j: (mod_idx(i), 0, 1)),
        pl.BlockSpec((None,) + w_tiles.shape[1:], lambda i, j: (layer, 0, 0, 0),
                     pipeline_mode=pl.Buffered(1)),
        pl.BlockSpec((TM_IN, LANES), rope_idx),
        pl.BlockSpec((TM_IN, LANES), rope_idx),
    ]
    args = [x, mod, mod, w_tiles, cos, sin]
    aliases = {}
    if prev is not None:
        in_specs += [pl.BlockSpec(memory_space=pl.ANY)] * 2
        args += list(prev)
        aliases = {len(args) - 2: 0, len(args) - 1: 1}
    v_width = OFF_Q - OFF_V
    return pl.pallas_call(
        functools.partial(_inproj_kernel, aliased=prev is not None),
        grid=(n_row_tiles, n_col_tiles),
        in_specs=in_specs,
        out_specs=[pl.BlockSpec((TM_IN, TN_IN), lambda i, j: (i + row_tile0, j)),
                   pl.BlockSpec((v_width, TM_IN), lambda i, j: (0, i + row_tile0))],
        out_shape=[jax.ShapeDtypeStruct((rows, N_IN), BF16),
                   jax.ShapeDtypeStruct((v_width, rows), BF16)],
        scratch_shapes=[pltpu.VMEM((TM_IN, d), BF16)],
        input_output_aliases=aliases,
        compiler_params=_cparams(("arbitrary", "arbitrary")),
        name="in_projection",
    )(*args)


_NT = (((1,), (1,)), ((), ()))


def _lam_value(lam_ref, lam_init):
    lq = lam_ref[...]
    return (jnp.exp(jnp.sum(lq[0:1] * lq[1:2], axis=1, keepdims=True))
            - jnp.exp(jnp.sum(lq[2:3] * lq[3:4], axis=1, keepdims=True)) + lam_init)


def _split_maps(q):
    lane = lax.broadcasted_iota(jnp.int32, q.shape, 1)
    first = (lane // (QK_DIM // 2)) % 2 == 0
    zero = jnp.zeros_like(q)
    return jnp.where(first, q, zero), jnp.where(first, zero, q)


def _head_norm(o, g_ref, lam_init):
    y = o * lax.rsqrt(jnp.mean(o * o, axis=-1, keepdims=True) + LN_EPS)
    return (y * g_ref[...] * (1.0 - lam_init)).astype(BF16)


def _attn_latent_kernel(lam_ref, g_ref, q_ref, kl_ref, kc_ref, vtl_ref, vtc_ref, o_ref,
                        s_even, m_even, s_odd, m_odd, *, lam_init, seq, ctx_len):
    t = pl.program_id(0)
    lam = _lam_value(lam_ref, lam_init)
    q1, q2 = _split_maps(q_ref[...])
    tq = q_ref.shape[0]

    @pl.when(t == 0)
    def _():
        s_odd[...] = jnp.zeros_like(s_odd)
        m_odd[...] = jnp.zeros_like(m_odd)

    ones_rows = jnp.ones((ONES_ROWS, ATT_KB), BF16)

    def step(s_w, m_w, s_r, m_r):
        n_lat_kb = seq // ATT_KB
        for c in range(tq // ATT_QC):
            qs = slice(c * ATT_QC, (c + 1) * ATT_QC)
            qc = (q1[qs, :], q2[qs, :])
            m_old = (m_r[0, 0:1, qs], m_r[1, 0:1, qs])
            m_new = [None, None]
            r = [None, None]
            for kb in range(n_lat_kb + ctx_len // ATT_KB):
                ks = slice(kb * ATT_KB, (kb + 1) * ATT_KB)
                if kb < n_lat_kb:
                    k_blk, vt_blk = kl_ref[ks, :], vtl_ref[:, ks]
                else:
                    cs = slice((kb - n_lat_kb) * ATT_KB, (kb - n_lat_kb + 1) * ATT_KB)
                    k_blk, vt_blk = kc_ref[cs, :], vtc_ref[:, cs]
                vt_ones = jnp.concatenate([vt_blk, ones_rows], axis=0)
                for mp in range(2):
                    sb = lax.dot_general(k_blk, qc[mp], _NT, preferred_element_type=F32)
                    s_w[mp, ks, qs] = sb
                    mb = jnp.max(sb, axis=0, keepdims=True)
                    m_new[mp] = mb if m_new[mp] is None else jnp.maximum(m_new[mp], mb)
                    e = jnp.exp2(s_r[mp, ks, qs] - m_old[mp]).astype(BF16)
                    part = jnp.dot(vt_ones, e, preferred_element_type=F32)
                    r[mp] = part if r[mp] is None else r[mp] + part
            for mp in range(2):
                m_w[mp, :, qs] = jnp.broadcast_to(m_new[mp], (SUBLANES, ATT_QC))
            r1, r2 = r
            ot = (r1[0:V_DIM] / r1[V_DIM:V_DIM + 1]
                  - lam * (r2[0:V_DIM] / r2[V_DIM:V_DIM + 1]))
            yt = ot * lax.rsqrt(jnp.mean(ot * ot, axis=0, keepdims=True) + LN_EPS)
            o_ref[qs, :] = (yt.T * g_ref[...] * (1.0 - lam_init)).astype(BF16)

    @pl.when(t % 2 == 0)
    def _():
        step(s_even, m_even, s_odd, m_odd)

    @pl.when(t % 2 == 1)
    def _():
        step(s_odd, m_odd, s_even, m_even)


def _attn_context_kernel(lam_ref, g_ref, q_ref, k_ref, v_ref, _, o_ref, *, lam_init):
    lam = _lam_value(lam_ref, lam_init)
    for h in range(ATT_HEADS):
        cols = slice(h * LANES, (h + 1) * LANES)
        q1, q2 = _split_maps(q_ref[:, cols])
        k = k_ref[:, cols]
        s1 = lax.dot_general(q1, k, _NT, preferred_element_type=F32)
        s2 = lax.dot_general(q2, k, _NT, preferred_element_type=F32)
        e1 = jnp.exp2(s1 - jnp.max(s1, axis=-1, keepdims=True))
        e2 = jnp.exp2(s2 - jnp.max(s2, axis=-1, keepdims=True))
        c1 = 1.0 / jnp.sum(e1, axis=-1, keepdims=True)
        c2 = lam / jnp.sum(e2, axis=-1, keepdims=True)
        a = (e1 * c1 - e2 * c2).astype(BF16)
        o = jnp.dot(a, v_ref[:, cols], preferred_element_type=F32)
        o_ref[:, cols] = _head_norm(o, g_ref, lam_init)


def _attention(p, vt, lam_qk, subln_g, layer, lam_init, n_batch, seq, ctx_len, with_context):
    rows = p.shape[0]
    n_q = seq // ATT_TQ
    n_tiles = n_batch * ATT_HEADS * n_q
    ctx_blk = n_batch * seq // ctx_len
    kcol, vcol, qcol = OFF_K // LANES, OFF_V // LANES, OFF_Q // LANES
    width = ATT_HEADS * V_DIM

    def decode(tile):
        qi = tile % n_q
        bh = tile // n_q
        return bh // ATT_HEADS, bh % ATT_HEADS, qi

    def first(t):
        return decode(jnp.minimum(t, n_tiles - 1))

    def second(t):
        return decode(jnp.maximum(t - 1, 0))

    def q_idx(t):
        b, h, qi = first(t)
        return (b * n_q + qi, qcol + h)

    def o_idx(t):
        b, h, qi = second(t)
        return (b * n_q + qi, h)

    def kv_idx(which, col0, blk0):
        def idx(t):
            b, h, _ = which(t)
            return (blk0 + b, col0 + h)
        return idx

    lam_spec = pl.BlockSpec((None, 4, QK_DIM), lambda *_: (layer, 0, 0))
    g_spec = pl.BlockSpec((None, 1, V_DIM), lambda *_: (layer, 0, 0))
    s_shape = pltpu.VMEM((2, seq + ctx_len, ATT_TQ), F32)
    m_shape = pltpu.VMEM((2, SUBLANES, ATT_TQ), F32)
    def vt_idx(blk0):
        def idx(t):
            b, h, _ = second(t)
            return (h, blk0 + b)
        return idx

    att = pl.pallas_call(
        functools.partial(_attn_latent_kernel, lam_init=lam_init, seq=seq, ctx_len=ctx_len),
        grid=(n_tiles + 1,),
        in_specs=[
            lam_spec, g_spec,
            pl.BlockSpec((ATT_TQ, LANES), q_idx),
            pl.BlockSpec((seq, LANES), kv_idx(first, kcol, 0)),
            pl.BlockSpec((ctx_len, LANES), kv_idx(first, kcol, ctx_blk)),
            pl.BlockSpec((V_DIM, seq), vt_idx(0)),
            pl.BlockSpec((V_DIM, ctx_len), vt_idx(ctx_blk)),
        ],
        out_specs=pl.BlockSpec((ATT_TQ, LANES), o_idx),
        out_shape=jax.ShapeDtypeStruct((rows, width), BF16),
        scratch_shapes=[s_shape, m_shape, s_shape, m_shape],
        compiler_params=_cparams(("arbitrary",)),
        name="diff_attention",
    )(lam_qk, subln_g, p, p, p, vt, vt)
    if not with_context:
        return att

    return pl.pallas_call(
        functools.partial(_attn_context_kernel, lam_init=lam_init),
        grid=(n_batch,),
        in_specs=[
            lam_spec, g_spec,
            pl.BlockSpec((ctx_len, width), lambda b: (ctx_blk + b, OFF_Q // width)),
            pl.BlockSpec((ctx_len, width), lambda b: (ctx_blk + b, OFF_K // width)),
            pl.BlockSpec((ctx_len, width), lambda b: (ctx_blk + b, OFF_V // width)),
            pl.BlockSpec(memory_space=pl.ANY),
        ],
        out_specs=pl.BlockSpec((ctx_len, width), lambda b: (ctx_blk + b, 0)),
        out_shape=jax.ShapeDtypeStruct((rows, width), BF16),
        input_output_aliases={5: 0},
        compiler_params=_cparams(("arbitrary",)),
        name="diff_attention_ctx",
    )(lam_qk, subln_g, p, p, p, att)


def _fourier_kernel(*refs, length, norm, aliased):
    if aliased:
        x_ref, cs_ref, w_ref, _, o_ref, xcs_ref = refs
    else:
        x_ref, cs_ref, w_ref, o_ref, xcs_ref = refs

    @pl.when(pl.program_id(1) == 0)
    def _():
        for g in range(FOURIER_GROUPS):
            t = jnp.dot(x_ref[:, g * GROUP_C:(g + 1) * GROUP_C], cs_ref[...],
                        preferred_element_type=F32)
            xcs_ref[0:length, g * GROUP_C:(g + 1) * GROUP_C] = t[:, :GROUP_C].astype(BF16)
            xcs_ref[length:2 * length, g * GROUP_C:(g + 1) * GROUP_C] = t[:, GROUP_C:].astype(BF16)

    y = jnp.dot(w_ref[...], xcs_ref[...], preferred_element_type=F32)
    o_ref[...] = (y * norm).astype(BF16)


def _fourier(p, cs, w_pos, length, row_blk0, n_batch, prev=None):
    rows = p.shape[0]
    width = FOURIER_GROUPS * GROUP_C
    tr = min(TR_FOUR, length)
    n_r = length // tr
    norm = 1.0 / math.sqrt(length * GROUP_C)
    in_specs = [
        pl.BlockSpec((length, width), lambda b, r: (row_blk0 + b, OFF_F // width)),
        pl.BlockSpec((GROUP_C, 2 * GROUP_C), lambda b, r: (0, 0)),
        pl.BlockSpec((tr, 2 * length), lambda b, r: (r, 0)),
    ]
    args = [p, cs, w_pos]
    aliases = {}
    if prev is not None:
        in_specs.append(pl.BlockSpec(memory_space=pl.ANY))
        args.append(prev)
        aliases = {3: 0}
    kern = functools.partial(_fourier_kernel, length=length, norm=norm, aliased=prev is not None)
    return pl.pallas_call(
        kern,
        grid=(n_batch, n_r),
        in_specs=in_specs,
        out_specs=pl.BlockSpec((tr, width), lambda b, r: ((row_blk0 + b) * n_r + r, 0)),
        out_shape=jax.ShapeDtypeStruct((rows, width), BF16),
        scratch_shapes=[pltpu.VMEM((2 * length, width), BF16)],
        input_output_aliases=aliases,
        compiler_params=_cparams(("arbitrary", "arbitrary")),
        name=f"fourier_{length}",
    )(*args)


def _pool_kernel(*refs, length, aliased):
    if aliased:
        x_ref, band_ref, wg_ref, sc_ref, _, o_ref, xp_ref = refs
    else:
        x_ref, band_ref, wg_ref, sc_ref, o_ref, xp_ref = refs
    width = x_ref.shape[1]
    t_rows = min(POOL_T, length)
    slab_rows = t_rows + 2 * POOL_PAD
    xp_ref[0:POOL_PAD, :] = jnp.zeros((POOL_PAD, width), BF16)
    xp_ref[POOL_PAD + length:2 * POOL_PAD + length, :] = jnp.zeros((POOL_PAD, width), BF16)
    xp_ref[POOL_PAD:POOL_PAD + length, :] = x_ref[...]
    for i in range(length // t_rows):
        r0 = i * t_rows
        t = r0 + lax.broadcasted_iota(jnp.int32, (t_rows, 1), 0)
        for g, w in enumerate(POOL_WINDOWS):
            lo = w // 2
            hi = w - lo
            cols = slice(g * GROUP_C, (g + 1) * GROUP_C)
            slab = xp_ref[r0:r0 + slab_rows, cols]
            sums = jnp.dot(band_ref[g], slab, preferred_element_type=F32)
            cnt = (jnp.minimum(t + hi, length) - jnp.maximum(t - lo, 0)).astype(F32)
            d = sums / cnt - x_ref[r0:r0 + t_rows, cols].astype(F32)
            y = jnp.dot(d.astype(BF16), wg_ref[g], preferred_element_type=F32)
            o_ref[r0:r0 + t_rows, cols] = (y * sc_ref[:, cols]).astype(BF16)


def _pool_bands(t_rows):
    tl = np.arange(t_rows)[:, None] + POOL_PAD
    jl = np.arange(t_rows + 2 * POOL_PAD)[None, :]
    bands = []
    for w in POOL_WINDOWS:
        lo = w // 2
        hi = w - lo
        bands.append(((jl >= tl - lo) & (jl < tl + hi)).astype(np.float32))
    return jnp.asarray(np.stack(bands), dtype=BF16)


def _pool(p, w_grp, scale, layer, length, row_blk0, n_batch, prev=None):
    rows = p.shape[0]
    width = len(POOL_WINDOWS) * GROUP_C
    t_rows = min(POOL_T, length)
    bands = _pool_bands(t_rows)
    in_specs = [
        pl.BlockSpec((length, width), lambda b: (row_blk0 + b, OFF_P // width)),
        pl.BlockSpec(bands.shape, lambda b: (0, 0, 0)),
        pl.BlockSpec((None, len(POOL_WINDOWS), GROUP_C, GROUP_C), lambda b: (layer, 0, 0, 0)),
        pl.BlockSpec((None, 1, width), lambda b: (layer, 0, 0)),
    ]
    args = [p, bands, w_grp, scale]
    aliases = {}
    if prev is not None:
        in_specs.append(pl.BlockSpec(memory_space=pl.ANY))
        args.append(prev)
        aliases = {4: 0}
    kern = functools.partial(_pool_kernel, length=length, aliased=prev is not None)
    return pl.pallas_call(
        kern,
        grid=(n_batch,),
        in_specs=in_specs,
        out_specs=pl.BlockSpec((length, width), lambda b: (row_blk0 + b, 0)),
        out_shape=jax.ShapeDtypeStruct((rows, width), BF16),
        scratch_shapes=[pltpu.VMEM((length + 2 * POOL_PAD, width), BF16)],
        input_output_aliases=aliases,
        compiler_params=_cparams(("arbitrary",)),
        name=f"pool_{length}",
    )(*args)


def _merge_kernel(att_ref, four_ref, pool_ref, g0_ref, g1_ref, g2_ref, x_ref, gm_ref,
                  lng_ref, lnb_ref, wa_ref, wf_ref, wp_ref, wo_ref, o_ref, *, alpha):
    ba = jnp.dot(att_ref[...], wa_ref[...], preferred_element_type=F32)
    bf = jnp.dot(four_ref[...], wf_ref[...], preferred_element_type=F32)
    bp = jnp.dot(pool_ref[...], wp_ref[...], preferred_element_type=F32)
    m = (g0_ref[...].astype(F32) * ba + g1_ref[...].astype(F32) * bf
         + g2_ref[...].astype(F32) * bp)
    y = jnp.dot(m.astype(BF16), wo_ref[...], preferred_element_type=F32)
    z = alpha * x_ref[...] + gm_ref[...] * y
    o_ref[...] = _layer_norm(z) * lng_ref[...] + lnb_ref[...]


def _merge(att, four, pool, p, x, mod, ln_g, ln_b, w_att, w_four, w_pool, w_out, layer,
           alpha, tiles_per_batch, n_batch, rows):
    d = x.shape[1]
    tm = TM_MIX
    gcol = OFF_G // d

    def mod_idx(i):
        return layer * MOD_ROWS + jnp.minimum(i // tiles_per_batch, n_batch)

    def const3(i):
        return (layer, 0, 0)

    return pl.pallas_call(
        functools.partial(_merge_kernel, alpha=alpha),
        grid=(rows // tm,),
        in_specs=[
            pl.BlockSpec((tm, att.shape[1]), lambda i: (i, 0)),
            pl.BlockSpec((tm, four.shape[1]), lambda i: (i, 0)),
            pl.BlockSpec((tm, pool.shape[1]), lambda i: (i, 0)),
            pl.BlockSpec((tm, d), lambda i: (i, gcol)),
            pl.BlockSpec((tm, d), lambda i: (i, gcol + 1)),
            pl.BlockSpec((tm, d), lambda i: (i, gcol + 2)),
            pl.BlockSpec((tm, d), lambda i: (i, 0)),
            pl.BlockSpec((None, 1, d), lambda i: (mod_idx(i), 0, 2)),
            pl.BlockSpec((None, 1, d), const3),
            pl.BlockSpec((None, 1, d), const3),
            pl.BlockSpec((None,) + w_att.shape[1:], const3),
            pl.BlockSpec((None,) + w_four.shape[1:], const3),
            pl.BlockSpec((None,) + w_pool.shape[1:], const3),
            pl.BlockSpec((None,) + w_out.shape[1:], const3),
        ],
        out_specs=pl.BlockSpec((tm, d), lambda i: (i, 0)),
        out_shape=jax.ShapeDtypeStruct((rows, d), F32),
        compiler_params=_cparams(("arbitrary",)),
        name="branch_merge",
    )(att, four, pool, p, p, p, x, mod, ln_g, ln_b, w_att, w_four, w_pool, w_out)


def _ffn_kernel(x_ref, sh_ref, sc_ref, gm_ref, lng_ref, lnb_ref, wg_ref, wu_ref, wd_ref,
                o_ref, *, alpha):
    x = x_ref[...]
    h = (_layer_norm(x) * (1.0 + sc_ref[...]) + sh_ref[...]).astype(BF16)
    d_ff = wg_ref.shape[1]
    acc = None
    for c in range(d_ff // TF_FFN):
        cols = slice(c * TF_FFN, (c + 1) * TF_FFN)
        gate = jnp.dot(h, wg_ref[:, cols], preferred_element_type=F32)
        up = jnp.dot(h, wu_ref[:, cols], preferred_element_type=F32)
        a = (gate * _sigmoid(gate) * up).astype(BF16)
        part = jnp.dot(a, wd_ref[cols, :], preferred_element_type=F32)
        acc = part if acc is None else acc + part
    z = alpha * x + gm_ref[...] * acc
    o_ref[...] = _layer_norm(z) * lng_ref[...] + lnb_ref[...]


def _ffn(x, mod, ln_g, ln_b, w_gate, w_up, w_down, layer, alpha, tiles_per_batch, n_batch):
    rows, d = x.shape
    d_ff = w_gate.shape[2]
    tm = TM_FFN
    assert d_ff % TF_FFN == 0

    def mod_idx(i):
        return layer * MOD_ROWS + jnp.minimum(i // tiles_per_batch, n_batch)

    def resident(shape):
        return pl.BlockSpec((None,) + shape, lambda i: (layer, 0, 0),
                            pipeline_mode=pl.Buffered(1))

    return pl.pallas_call(
        functools.partial(_ffn_kernel, alpha=alpha),
        grid=(rows // tm,),
        in_specs=[
            pl.BlockSpec((tm, d), lambda i: (i, 0)),
            pl.BlockSpec((None, 1, d), lambda i: (mod_idx(i), 0, 3)),
            pl.BlockSpec((None, 1, d), lambda i: (mod_idx(i), 0, 4)),
            pl.BlockSpec((None, 1, d), lambda i: (mod_idx(i), 0, 5)),
            pl.BlockSpec((None, 1, d), lambda i: (layer, 0, 0)),
            pl.BlockSpec((None, 1, d), lambda i: (layer, 0, 0)),
            resident((d, d_ff)),
            resident((d, d_ff)),
            resident((d_ff, d)),
        ],
        out_specs=pl.BlockSpec((tm, d), lambda i: (i, 0)),
        out_shape=jax.ShapeDtypeStruct((rows, d), F32),
        compiler_params=_cparams(("arbitrary",)),
        name="swiglu",
    )(x, mod, mod, mod, ln_g, ln_b, w_gate, w_up, w_down)


def _rope_layout(w):
    lead = w.shape[:-1]
    w = w.reshape(lead + (ATT_HEADS, 2, 2, 2, ROPE_FREQS))
    nd = len(lead)
    perm = tuple(range(nd)) + (nd, nd + 3, nd + 1, nd + 2, nd + 4)
    return w.transpose(perm).reshape(lead + (ATT_HEADS * 2 * QK_DIM,))


def _rope_tables(seq):
    rows = seq // GRID_W
    row = np.repeat(np.arange(rows), GRID_W).astype(np.float32)
    col = np.tile(np.arange(GRID_W), rows).astype(np.float32)
    inv = (ROPE_BASE ** (-np.arange(ROPE_FREQS, dtype=np.float32) / ROPE_FREQS)).astype(np.float32)
    ar = row[:, None] * inv[None, :]
    ac = col[:, None] * inv[None, :]
    ang = np.concatenate([ar, ac, ar, ac], axis=-1).astype(np.float32)
    cos = np.cos(ang).astype(np.float32)
    sin = np.sin(ang).astype(np.float32)
    cos = np.concatenate([cos, cos], axis=-1)
    sin = np.concatenate([-sin, sin], axis=-1)

    def with_identity(t, fill):
        ident = np.full((TM_IN, LANES), fill, np.float32)
        return jnp.asarray(np.concatenate([t, ident], axis=0), dtype=F32)

    return with_identity(cos, 1.0), with_identity(sin, 0.0)


def _dft_tables(length):
    def cos_sin(n):
        k = np.arange(n, dtype=np.int64)
        ang = 2.0 * np.pi * ((k[:, None] * k[None, :]) % n).astype(np.float64) / n
        return np.cos(ang), np.sin(ang)

    cc, sc = cos_sin(GROUP_C)
    cl, sl = cos_sin(length)
    cs = jnp.asarray(np.concatenate([cc, sc], axis=1), dtype=F32).astype(BF16)
    w_pos = jnp.asarray(np.concatenate([cl, -sl], axis=1), dtype=F32).astype(BF16)
    return cs, w_pos


def kernel(x, c, ctx, c_ctx, w_mod, b_mod, w_in, lam_qk, subln_g, w_att_br, w_four_br,
           w_pool_grp, pool_scale, w_pool_br, w_out, ln1_g, ln1_b, w_ffn_gate, w_ffn_up,
           w_ffn_down, ln2_g, ln2_b):
    n_batch, seq, d = x.shape
    ctx_len = ctx.shape[1]
    depth = w_mod.shape[0]
    assert d == D_MODEL and seq % ATT_TQ == 0 and seq % TM_IN == 0 and n_batch < MOD_ROWS
    assert (n_batch * ctx_len) % TM_IN == 0 and seq % ctx_len == 0
    alpha = (2 * depth) ** 0.25

    xs = jnp.concatenate([x.reshape(n_batch * seq, d), ctx.reshape(n_batch * ctx_len, d)], axis=0)

    cond = jnp.zeros((MOD_ROWS, d), F32).at[:n_batch].set(c).at[n_batch].set(c_ctx)
    mod = _mod_vectors(cond, w_mod, b_mod).reshape(depth * MOD_ROWS, 1, 6 * d)

    bf = lambda w: w.astype(BF16)
    w_att_b, w_four_b, w_pool_b, w_out_b = map(bf, (w_att_br, w_four_br, w_pool_br, w_out))
    w_grp_b, w_g_b, w_u_b, w_d_b = map(bf, (w_pool_grp, w_ffn_gate, w_ffn_up, w_ffn_down))
    w_in_b = bf(w_in)
    w_tiles = [w_in_b[..., j * TN_IN:(j + 1) * TN_IN] for j in range(N_IN // TN_IN)]
    for j in (OFF_K // TN_IN, OFF_Q // TN_IN):
        w_tiles[j] = _rope_layout(w_tiles[j])
    w_tiles = jnp.stack(w_tiles, axis=1)
    vec3 = lambda v: v.reshape(depth, 1, v.shape[-1])
    subln3, pscale3 = vec3(subln_g), vec3(pool_scale)
    ln1_g3, ln1_b3, ln2_g3, ln2_b3 = map(vec3, (ln1_g, ln1_b, ln2_g, ln2_b))

    rope = _rope_tables(seq)
    cs_lat, wpos_lat = _dft_tables(seq)
    cs_ctx, wpos_ctx = _dft_tables(ctx_len)

    n_lat_rows = n_batch * seq
    ctx_blk0 = n_lat_rows // ctx_len

    n_lat_tiles = n_lat_rows // TM_IN
    for l in range(depth):
        last = l == depth - 1
        lam_init = 0.8 - 0.6 * math.exp(-0.3 * l)
        proj = functools.partial(_in_projection, xs, mod, w_tiles, rope, l, n_lat_tiles,
                                 seq // TM_IN, n_batch)
        if last:
            pv = proj(n_row_tiles=n_lat_tiles)
            p, vt = proj(row_tile0=n_lat_tiles, n_row_tiles=xs.shape[0] // TM_IN - n_lat_tiles,
                         n_col_tiles=OFF_Q // TN_IN, prev=pv)
        else:
            p, vt = proj()
        att = _attention(p, vt, lam_qk, subln3, l, lam_init, n_batch, seq, ctx_len,
                         with_context=not last)
        four = _fourier(p, cs_lat, wpos_lat, seq, 0, n_batch)
        pool = _pool(p, w_grp_b, pscale3, l, seq, 0, n_batch)
        if not last:
            four = _fourier(p, cs_ctx, wpos_ctx, ctx_len, ctx_blk0, n_batch, prev=four)
            pool = _pool(p, w_grp_b, pscale3, l, ctx_len, ctx_blk0, n_batch, prev=pool)
        xs = _merge(att, four, pool, p, xs, mod, ln1_g3, ln1_b3, w_att_b, w_four_b, w_pool_b,
                    w_out_b, l, alpha, seq // TM_MIX, n_batch,
                    rows=n_lat_rows if last else xs.shape[0])
        xs = _ffn(xs, mod, ln2_g3, ln2_b3, w_g_b, w_u_b, w_d_b, l, alpha, seq // TM_FFN, n_batch)

    return xs.reshape(n_batch, seq, d)
```

```python
import functools
import math

import jax
import jax.numpy as jnp
import numpy as np
from jax import lax
from jax.experimental import pallas as pl
from jax.experimental.pallas import tpu as pltpu

F32 = jnp.float32
BF16 = jnp.bfloat16

D_MODEL = 1024
GRID_W = 64
ATT_HEADS = 8
QK_DIM = 64
V_DIM = 128
ROPE_FREQS = 16
ROPE_BASE = 10000.0
FOURIER_GROUPS = 4
GROUP_C = 128
POOL_WINDOWS = (2, 4, 8, 16)
OFF_K, OFF_V, OFF_Q, OFF_F, OFF_P, OFF_G = 0, 1024, 2048, 3072, 3584, 4096
N_IN = 7168
LN_EPS = 1e-5
LOG2_E = math.log2(math.e)

LANES = 128
SUBLANES = 8
ONES_ROWS = 16
VMEM_LIMIT = 56 * 1024 * 1024

TM_IN = 1024
TN_IN = 1024
TC_IN = 256
ATT_TQ = 1024
ATT_QC = 256
ATT_KB = 256
TM_MIX = 512
TM_FFN = 512
TF_FFN = 256
TR_FOUR = 512
POOL_T = 256
POOL_PAD = 128
MOD_ROWS = 16


def _cparams(sem):
    return pltpu.CompilerParams(dimension_semantics=sem, vmem_limit_bytes=VMEM_LIMIT)


def _layer_norm(x):
    mu = jnp.mean(x, axis=-1, keepdims=True)
    xc = x - mu
    var = jnp.mean(xc * xc, axis=-1, keepdims=True)
    return xc * lax.rsqrt(var + LN_EPS)


def _sigmoid(x):
    return 0.5 * jnp.tanh(0.5 * x) + 0.5


def _mod_kernel(c_ref, w_ref, b_ref, o_ref):
    c = c_ref[...]
    sc = (c * _sigmoid(c)).astype(BF16)
    o_ref[...] = jnp.dot(sc, w_ref[...].astype(BF16), preferred_element_type=F32) + b_ref[...]


def _mod_vectors(cond, w_mod, b_mod):
    depth, d, n = w_mod.shape
    tn = 1024
    return pl.pallas_call(
        _mod_kernel,
        grid=(depth, n // tn),
        in_specs=[
            pl.BlockSpec((MOD_ROWS, d), lambda l, j: (0, 0)),
            pl.BlockSpec((None, d, tn), lambda l, j: (l, 0, j)),
            pl.BlockSpec((None, 1, tn), lambda l, j: (l, 0, j)),
        ],
        out_specs=pl.BlockSpec((None, MOD_ROWS, tn), lambda l, j: (l, 0, j)),
        out_shape=jax.ShapeDtypeStruct((depth, MOD_ROWS, n), F32),
        compiler_params=_cparams(("arbitrary", "arbitrary")),
        name="mod_vectors",
    )(cond, w_mod, b_mod.reshape(depth, 1, n))


def _inproj_kernel(*refs, aliased):
    if aliased:
        x_ref, sh_ref, sc_ref, w_ref, cos_ref, sin_ref, _, _, o_ref, vt_ref, u_ref = refs
    else:
        x_ref, sh_ref, sc_ref, w_ref, cos_ref, sin_ref, o_ref, vt_ref, u_ref = refs
    j = pl.program_id(1)
    is_k = j == OFF_K // TN_IN
    is_q = j == OFF_Q // TN_IN
    is_gate = j >= OFF_G // TN_IN

    def project(epilogue, first=False):
        if first:
            h = _layer_norm(x_ref[...])
            u_ref[...] = (h * (1.0 + sc_ref[...]) + sh_ref[...]).astype(BF16)
        for c in range(TN_IN // TC_IN):
            cols = slice(c * TC_IN, (c + 1) * TC_IN)
            acc = jnp.dot(u_ref[...], w_ref[j, :, cols], preferred_element_type=F32)
            epilogue(acc, cols)

    def rope(acc, cols):
        scale = jnp.where(is_q, QK_DIM ** -0.5 * LOG2_E, 1.0).astype(F32)
        cos, sin = cos_ref[...], sin_ref[...]
        for v in range(TC_IN // LANES):
            xs = acc[:, v * LANES:(v + 1) * LANES]
            r = xs * cos + pltpu.roll(xs, LANES // 2, 1) * sin
            lo = cols.start + v * LANES
            o_ref[:, lo:lo + LANES] = (r * scale).astype(BF16)

    def gate(acc, cols):
        o_ref[:, cols] = _sigmoid(acc).astype(BF16)

    def plain(acc, cols):
        o_ref[:, cols] = acc.astype(BF16)

    def values(acc, cols):
        o_ref[:, cols] = acc.astype(BF16)
        vt_ref[cols, :] = acc.T.astype(BF16)

    is_v = j == OFF_V // TN_IN
    assert OFF_K == 0
    pl.when(is_k)(lambda: project(rope, first=True))
    pl.when(is_q)(lambda: project(rope))
    pl.when(is_gate)(lambda: project(gate))
    pl.when(is_v)(lambda: project(values))
    pl.when(jnp.logical_not(is_k | is_q | is_gate | is_v))(lambda: project(plain))


def _in_projection(x, mod, w_tiles, rope, layer, n_lat_tiles, tiles_per_batch, n_batch,
                   row_tile0=0, n_row_tiles=None, n_col_tiles=None, prev=None):
    rows, d = x.shape
    cos, sin = rope
    n_rope_blocks = cos.shape[0] // TM_IN - 1
    n_row_tiles = rows // TM_IN if n_row_tiles is None else n_row_tiles
    n_col_tiles = N_IN // TN_IN if n_col_tiles is None else n_col_tiles

    def mod_idx(i):
        return layer * MOD_ROWS + jnp.minimum((i + row_tile0) // tiles_per_batch, n_batch)

    def rope_idx(i, j):
        it = i + row_tile0
        return (jnp.where(it < n_lat_tiles, it % tiles_per_batch, n_rope_blocks), 0)

    in_specs = [
        pl.BlockSpec((TM_IN, d), lambda i, j: (i + row_tile0, 0)),
        pl.BlockSpec((None, 1, d), lambda i, j: (mod_idx(i), 0, 0)),
        pl.BlockSpec((None, 1, d), lambda i, j: (mod_idx(i), 0, 1)),
        pl.BlockSpec((None,) + w_tiles.shape[1:], lambda i, j: (layer, 0, 0, 0),
                     pipeline_mode=pl.Buffered(1)),
        pl.BlockSpec((TM_IN, LANES), rope_idx),
        pl.BlockSpec((TM_IN, LANES), rope_idx),
    ]
    args = [x, mod, mod, w_tiles, cos, sin]
    aliases = {}
    if prev is not None:
        in_specs += [pl.BlockSpec(memory_space=pl.ANY)] * 2
        args += list(prev)
        aliases = {len(args) - 2: 0, len(args) - 1: 1}
    v_width = OFF_Q - OFF_V
    return pl.pallas_call(
        functools.partial(_inproj_kernel, aliased=prev is not None),
        grid=(n_row_tiles, n_col_tiles),
        in_specs=in_specs,
        out_specs=[pl.BlockSpec((TM_IN, TN_IN), lambda i, j: (i + row_tile0, j)),
                   pl.BlockSpec((v_width, TM_IN), lambda i, j: (0, i + row_tile0))],
        out_shape=[jax.ShapeDtypeStruct((rows, N_IN), BF16),
                   jax.ShapeDtypeStruct((v_width, rows), BF16)],
        scratch_shapes=[pltpu.VMEM((TM_IN, d), BF16)],
        input_output_aliases=aliases,
        compiler_params=_cparams(("arbitrary", "arbitrary")),
        name="in_projection",
    )(*args)


_NT = (((1,), (1,)), ((), ()))


def _lam_value(lam_ref, lam_init):
    lq = lam_ref[...]
    return (jnp.exp(jnp.sum(lq[0:1] * lq[1:2], axis=1, keepdims=True))
            - jnp.exp(jnp.sum(lq[2:3] * lq[3:4], axis=1, keepdims=True)) + lam_init)


def _split_maps(q):
    lane = lax.broadcasted_iota(jnp.int32, q.shape, 1)
    first = (lane // (QK_DIM // 2)) % 2 == 0
    zero = jnp.zeros_like(q)
    return jnp.where(first, q, zero), jnp.where(first, zero, q)


def _head_norm(o, g_ref, lam_init):
    y = o * lax.rsqrt(jnp.mean(o * o, axis=-1, keepdims=True) + LN_EPS)
    return (y * g_ref[...] * (1.0 - lam_init)).astype(BF16)


def _attn_latent_kernel(lam_ref, g_ref, q_ref, kl_ref, kc_ref, vtl_ref, vtc_ref, o_ref,
                        s_even, m_even, s_odd, m_odd, *, lam_init, seq, ctx_len):
    t = pl.program_id(0)
    lam = _lam_value(lam_ref, lam_init)
    q1, q2 = _split_maps(q_ref[...])
    tq = q_ref.shape[0]

    @pl.when(t == 0)
    def _():
        s_odd[...] = jnp.zeros_like(s_odd)
        m_odd[...] = jnp.zeros_like(m_odd)

    ones_rows = jnp.ones((ONES_ROWS, ATT_KB), BF16)

    def step(s_w, m_w, s_r, m_r):
        n_lat_kb = seq // ATT_KB
        for c in range(tq // ATT_QC):
            qs = slice(c * ATT_QC, (c + 1) * ATT_QC)
            qc = (q1[qs, :], q2[qs, :])
            m_old = (m_r[0, 0:1, qs], m_r[1, 0:1, qs])
            m_new = [None, None]
            r = [None, None]
            for kb in range(n_lat_kb + ctx_len // ATT_KB):
                ks = slice(kb * ATT_KB, (kb + 1) * ATT_KB)
                if kb < n_lat_kb:
                    k_blk, vt_blk = kl_ref[ks, :], vtl_ref[:, ks]
                else:
                    cs = slice((kb - n_lat_kb) * ATT_KB, (kb - n_lat_kb + 1) * ATT_KB)
                    k_blk, vt_blk = kc_ref[cs, :], vtc_ref[:, cs]
                vt_ones = jnp.concatenate([vt_blk, ones_rows], axis=0)
                for mp in range(2):
                    sb = lax.dot_general(k_blk, qc[mp], _NT, preferred_element_type=F32)
                    s_w[mp, ks, qs] = sb
                    mb = jnp.max(sb, axis=0, keepdims=True)
                    m_new[mp] = mb if m_new[mp] is None else jnp.maximum(m_new[mp], mb)
                    e = jnp.exp2(s_r[mp, ks, qs] - m_old[mp]).astype(BF16)
                    part = jnp.dot(vt_ones, e, preferred_element_type=F32)
                    r[mp] = part if r[mp] is None else r[mp] + part
            for mp in range(2):
                m_w[mp, :, qs] = jnp.broadcast_to(m_new[mp], (SUBLANES, ATT_QC))
            r1, r2 = r
            ot = (r1[0:V_DIM] / r1[V_DIM:V_DIM + 1]
                  - lam * (r2[0:V_DIM] / r2[V_DIM:V_DIM + 1]))
            yt = ot * lax.rsqrt(jnp.mean(ot * ot, axis=0, keepdims=True) + LN_EPS)
            o_ref[qs, :] = (yt.T * g_ref[...] * (1.0 - lam_init)).astype(BF16)

    @pl.when(t % 2 == 0)
    def _():
        step(s_even, m_even, s_odd, m_odd)

    @pl.when(t % 2 == 1)
    def _():
        step(s_odd, m_odd, s_even, m_even)


def _attn_context_kernel(lam_ref, g_ref, q_ref, k_ref, v_ref, _, o_ref, *, lam_init):
    lam = _lam_value(lam_ref, lam_init)
    for h in range(ATT_HEADS):
        cols = slice(h * LANES, (h + 1) * LANES)
        q1, q2 = _split_maps(q_ref[:, cols])
        k = k_ref[:, cols]
        s1 = lax.dot_general(q1, k, _NT, preferred_element_type=F32)
        s2 = lax.dot_general(q2, k, _NT, preferred_element_type=F32)
        e1 = jnp.exp2(s1 - jnp.max(s1, axis=-1, keepdims=True))
        e2 = jnp.exp2(s2 - jnp.max(s2, axis=-1, keepdims=True))
        c1 = 1.0 / jnp.sum(e1, axis=-1, keepdims=True)
        c2 = lam / jnp.sum(e2, axis=-1, keepdims=True)
        a = (e1 * c1 - e2 * c2).astype(BF16)
        o = jnp.dot(a, v_ref[:, cols], preferred_element_type=F32)
        o_ref[:, cols] = _head_norm(o, g_ref, lam_init)


def _attention(p, vt, lam_qk, subln_g, layer, lam_init, n_batch, seq, ctx_len, with_context):
    rows = p.shape[0]
    n_q = seq // ATT_TQ
    n_tiles = n_batch * ATT_HEADS * n_q
    ctx_blk = n_batch * seq // ctx_len
    kcol, vcol, qcol = OFF_K // LANES, OFF_V // LANES, OFF_Q // LANES
    width = ATT_HEADS * V_DIM

    def decode(tile):
        qi = tile % n_q
        bh = tile // n_q
        return bh // ATT_HEADS, bh % ATT_HEADS, qi

    def first(t):
        return decode(jnp.minimum(t, n_tiles - 1))

    def second(t):
        return decode(jnp.maximum(t - 1, 0))

    def q_idx(t):
        b, h, qi = first(t)
        return (b * n_q + qi, qcol + h)

    def o_idx(t):
        b, h, qi = second(t)
        return (b * n_q + qi, h)

    def kv_idx(which, col0, blk0):
        def idx(t):
            b, h, _ = which(t)
            return (blk0 + b, col0 + h)
        return idx

    lam_spec = pl.BlockSpec((None, 4, QK_DIM), lambda *_: (layer, 0, 0))
    g_spec = pl.BlockSpec((None, 1, V_DIM), lambda *_: (layer, 0, 0))
    s_shape = pltpu.VMEM((2, seq + ctx_len, ATT_TQ), F32)
    m_shape = pltpu.VMEM((2, SUBLANES, ATT_TQ), F32)
    def vt_idx(blk0):
        def idx(t):
            b, h, _ = second(t)
            return (h, blk0 + b)
        return idx

    att = pl.pallas_call(
        functools.partial(_attn_latent_kernel, lam_init=lam_init, seq=seq, ctx_len=ctx_len),
        grid=(n_tiles + 1,),
        in_specs=[
            lam_spec, g_spec,
            pl.BlockSpec((ATT_TQ, LANES), q_idx),
            pl.BlockSpec((seq, LANES), kv_idx(first, kcol, 0)),
            pl.BlockSpec((ctx_len, LANES), kv_idx(first, kcol, ctx_blk)),
            pl.BlockSpec((V_DIM, seq), vt_idx(0)),
            pl.BlockSpec((V_DIM, ctx_len), vt_idx(ctx_blk)),
        ],
        out_specs=pl.BlockSpec((ATT_TQ, LANES), o_idx),
        out_shape=jax.ShapeDtypeStruct((rows, width), BF16),
        scratch_shapes=[s_shape, m_shape, s_shape, m_shape],
        compiler_params=_cparams(("arbitrary",)),
        name="diff_attention",
    )(lam_qk, subln_g, p, p, p, vt, vt)
    if not with_context:
        return att

    return pl.pallas_call(
        functools.partial(_attn_context_kernel, lam_init=lam_init),
        grid=(n_batch,),
        in_specs=[
            lam_spec, g_spec,
            pl.BlockSpec((ctx_len, width), lambda b: (ctx_blk + b, OFF_Q // width)),
            pl.BlockSpec((ctx_len, width), lambda b: (ctx_blk + b, OFF_K // width)),
            pl.BlockSpec((ctx_len, width), lambda b: (ctx_blk + b, OFF_V // width)),
            pl.BlockSpec(memory_space=pl.ANY),
        ],
        out_specs=pl.BlockSpec((ctx_len, width), lambda b: (ctx_blk + b, 0)),
        out_shape=jax.ShapeDtypeStruct((rows, width), BF16),
        input_output_aliases={5: 0},
        compiler_params=_cparams(("arbitrary",)),
        name="diff_attention_ctx",
    )(lam_qk, subln_g, p, p, p, att)


def _fourier_kernel(*refs, length, norm, aliased):
    if aliased:
        x_ref, cs_ref, w_ref, _, o_ref, xcs_ref = refs
    else:
        x_ref, cs_ref, w_ref, o_ref, xcs_ref = refs

    @pl.when(pl.program_id(1) == 0)
    def _():
        for g in range(FOURIER_GROUPS):
            t = jnp.dot(x_ref[:, g * GROUP_C:(g + 1) * GROUP_C], cs_ref[...],
                        preferred_element_type=F32)
            xcs_ref[0:length, g * GROUP_C:(g + 1) * GROUP_C] = t[:, :GROUP_C].astype(BF16)
            xcs_ref[length:2 * length, g * GROUP_C:(g + 1) * GROUP_C] = t[:, GROUP_C:].astype(BF16)

    tr = o_ref.shape[0]
    row0 = pl.multiple_of(pl.program_id(1) * tr, tr)
    y = jnp.dot(w_ref[pl.ds(row0, tr), :], xcs_ref[...], preferred_element_type=F32)
    o_ref[...] = (y * norm).astype(BF16)


def _fourier(p, cs, w_pos, length, row_blk0, n_batch, prev=None):
    rows = p.shape[0]
    width = FOURIER_GROUPS * GROUP_C
    tr = min(TR_FOUR, length)
    n_r = length // tr
    norm = 1.0 / math.sqrt(length * GROUP_C)
    in_specs = [
        pl.BlockSpec((length, width), lambda b, r: (row_blk0 + b, OFF_F // width)),
        pl.BlockSpec((GROUP_C, 2 * GROUP_C), lambda b, r: (0, 0)),
        pl.BlockSpec((length, 2 * length), lambda b, r: (0, 0), pipeline_mode=pl.Buffered(1)),
    ]
    args = [p, cs, w_pos]
    aliases = {}
    if prev is not None:
        in_specs.append(pl.BlockSpec(memory_space=pl.ANY))
        args.append(prev)
        aliases = {3: 0}
    kern = functools.partial(_fourier_kernel, length=length, norm=norm, aliased=prev is not None)
    return pl.pallas_call(
        kern,
        grid=(n_batch, n_r),
        in_specs=in_specs,
        out_specs=pl.BlockSpec((tr, width), lambda b, r: ((row_blk0 + b) * n_r + r, 0)),
        out_shape=jax.ShapeDtypeStruct((rows, width), BF16),
        scratch_shapes=[pltpu.VMEM((2 * length, width), BF16)],
        input_output_aliases=aliases,
        compiler_params=_cparams(("arbitrary", "arbitrary")),
        name=f"fourier_{length}",
    )(*args)


def _pool_kernel(*refs, length, aliased):
    if aliased:
        x_ref, band_ref, wg_ref, sc_ref, _, o_ref, xp_ref = refs
    else:
        x_ref, band_ref, wg_ref, sc_ref, o_ref, xp_ref = refs
    width = x_ref.shape[1]
    t_rows = min(POOL_T, length)
    slab_rows = t_rows + 2 * POOL_PAD
    xp_ref[0:POOL_PAD, :] = jnp.zeros((POOL_PAD, width), BF16)
    xp_ref[POOL_PAD + length:2 * POOL_PAD + length, :] = jnp.zeros((POOL_PAD, width), BF16)
    xp_ref[POOL_PAD:POOL_PAD + length, :] = x_ref[...]
    for i in range(length // t_rows):
        r0 = i * t_rows
        t = r0 + lax.broadcasted_iota(jnp.int32, (t_rows, 1), 0)
        for g, w in enumerate(POOL_WINDOWS):
            lo = w // 2
            hi = w - lo
            cols = slice(g * GROUP_C, (g + 1) * GROUP_C)
            slab = xp_ref[r0:r0 + slab_rows, cols]
            sums = jnp.dot(band_ref[g], slab, preferred_element_type=F32)
            cnt = (jnp.minimum(t + hi, length) - jnp.maximum(t - lo, 0)).astype(F32)
            d = sums / cnt - x_ref[r0:r0 + t_rows, cols].astype(F32)
            y = jnp.dot(d.astype(BF16), wg_ref[g], preferred_element_type=F32)
            o_ref[r0:r0 + t_rows, cols] = (y * sc_ref[:, cols]).astype(BF16)


def _pool_bands(t_rows):
    tl = np.arange(t_rows)[:, None] + POOL_PAD
    jl = np.arange(t_rows + 2 * POOL_PAD)[None, :]
    bands = []
    for w in POOL_WINDOWS:
        lo = w // 2
        hi = w - lo
        bands.append(((jl >= tl - lo) & (jl < tl + hi)).astype(np.float32))
    return jnp.asarray(np.stack(bands), dtype=BF16)


def _pool(p, w_grp, scale, layer, length, row_blk0, n_batch, prev=None):
    rows = p.shape[0]
    width = len(POOL_WINDOWS) * GROUP_C
    t_rows = min(POOL_T, length)
    bands = _pool_bands(t_rows)
    in_specs = [
        pl.BlockSpec((length, width), lambda b: (row_blk0 + b, OFF_P // width)),
        pl.BlockSpec(bands.shape, lambda b: (0, 0, 0)),
        pl.BlockSpec((None, len(POOL_WINDOWS), GROUP_C, GROUP_C), lambda b: (layer, 0, 0, 0)),
        pl.BlockSpec((None, 1, width), lambda b: (layer, 0, 0)),
    ]
    args = [p, bands, w_grp, scale]
    aliases = {}
    if prev is not None:
        in_specs.append(pl.BlockSpec(memory_space=pl.ANY))
        args.append(prev)
        aliases = {4: 0}
    kern = functools.partial(_pool_kernel, length=length, aliased=prev is not None)
    return pl.pallas_call(
        kern,
        grid=(n_batch,),
        in_specs=in_specs,
        out_specs=pl.BlockSpec((length, width), lambda b: (row_blk0 + b, 0)),
        out_shape=jax.ShapeDtypeStruct((rows, width), BF16),
        scratch_shapes=[pltpu.VMEM((length + 2 * POOL_PAD, width), BF16)],
        input_output_aliases=aliases,
        compiler_params=_cparams(("arbitrary",)),
        name=f"pool_{length}",
    )(*args)


def _merge_kernel(att_ref, four_ref, pool_ref, g0_ref, g1_ref, g2_ref, x_ref, gm_ref,
                  lng_ref, lnb_ref, wa_ref, wf_ref, wp_ref, wo_ref, o_ref, *, alpha):
    ba = jnp.dot(att_ref[...], wa_ref[...], preferred_element_type=F32)
    bf = jnp.dot(four_ref[...], wf_ref[...], preferred_element_type=F32)
    bp = jnp.dot(pool_ref[...], wp_ref[...], preferred_element_type=F32)
    m = (g0_ref[...].astype(F32) * ba + g1_ref[...].astype(F32) * bf
         + g2_ref[...].astype(F32) * bp)
    y = jnp.dot(m.astype(BF16), wo_ref[...], preferred_element_type=F32)
    z = alpha * x_ref[...] + gm_ref[...] * y
    o_ref[...] = _layer_norm(z) * lng_ref[...] + lnb_ref[...]


def _merge(att, four, pool, p, x, mod, ln_g, ln_b, w_att, w_four, w_pool, w_out, layer,
           alpha, tiles_per_batch, n_batch, rows):
    d = x.shape[1]
    tm = TM_MIX
    gcol = OFF_G // d

    def mod_idx(i):
        return layer * MOD_ROWS + jnp.minimum(i // tiles_per_batch, n_batch)

    def const3(i):
        return (layer, 0, 0)

    return pl.pallas_call(
        functools.partial(_merge_kernel, alpha=alpha),
        grid=(rows // tm,),
        in_specs=[
            pl.BlockSpec((tm, att.shape[1]), lambda i: (i, 0)),
            pl.BlockSpec((tm, four.shape[1]), lambda i: (i, 0)),
            pl.BlockSpec((tm, pool.shape[1]), lambda i: (i, 0)),
            pl.BlockSpec((tm, d), lambda i: (i, gcol)),
            pl.BlockSpec((tm, d), lambda i: (i, gcol + 1)),
            pl.BlockSpec((tm, d), lambda i: (i, gcol + 2)),
            pl.BlockSpec((tm, d), lambda i: (i, 0)),
            pl.BlockSpec((None, 1, d), lambda i: (mod_idx(i), 0, 2)),
            pl.BlockSpec((None, 1, d), const3),
            pl.BlockSpec((None, 1, d), const3),
            pl.BlockSpec((None,) + w_att.shape[1:], const3),
            pl.BlockSpec((None,) + w_four.shape[1:], const3),
            pl.BlockSpec((None,) + w_pool.shape[1:], const3),
            pl.BlockSpec((None,) + w_out.shape[1:], const3),
        ],
        out_specs=pl.BlockSpec((tm, d), lambda i: (i, 0)),
        out_shape=jax.ShapeDtypeStruct((rows, d), F32),
        compiler_params=_cparams(("arbitrary",)),
        name="branch_merge",
    )(att, four, pool, p, p, p, x, mod, ln_g, ln_b, w_att, w_four, w_pool, w_out)


def _ffn_kernel(x_ref, sh_ref, sc_ref, gm_ref, lng_ref, lnb_ref, wg_ref, wu_ref, wd_ref,
                o_ref, *, alpha):
    x = x_ref[...]
    h = (_layer_norm(x) * (1.0 + sc_ref[...]) + sh_ref[...]).astype(BF16)
    d_ff = wg_ref.shape[1]
    acc = None
    for c in range(d_ff // TF_FFN):
        cols = slice(c * TF_FFN, (c + 1) * TF_FFN)
        gate = jnp.dot(h, wg_ref[:, cols], preferred_element_type=F32)
        up = jnp.dot(h, wu_ref[:, cols], preferred_element_type=F32)
        a = (gate * _sigmoid(gate) * up).astype(BF16)
        part = jnp.dot(a, wd_ref[cols, :], preferred_element_type=F32)
        acc = part if acc is None else acc + part
    z = alpha * x + gm_ref[...] * acc
    o_ref[...] = _layer_norm(z) * lng_ref[...] + lnb_ref[...]


def _ffn(x, mod, ln_g, ln_b, w_gate, w_up, w_down, layer, alpha, tiles_per_batch, n_batch):
    rows, d = x.shape
    d_ff = w_gate.shape[2]
    tm = TM_FFN
    assert d_ff % TF_FFN == 0

    def mod_idx(i):
        return layer * MOD_ROWS + jnp.minimum(i // tiles_per_batch, n_batch)

    def resident(shape):
        return pl.BlockSpec((None,) + shape, lambda i: (layer, 0, 0),
                            pipeline_mode=pl.Buffered(1))

    return pl.pallas_call(
        functools.partial(_ffn_kernel, alpha=alpha),
        grid=(rows // tm,),
        in_specs=[
            pl.BlockSpec((tm, d), lambda i: (i, 0)),
            pl.BlockSpec((None, 1, d), lambda i: (mod_idx(i), 0, 3)),
            pl.BlockSpec((None, 1, d), lambda i: (mod_idx(i), 0, 4)),
            pl.BlockSpec((None, 1, d), lambda i: (mod_idx(i), 0, 5)),
            pl.BlockSpec((None, 1, d), lambda i: (layer, 0, 0)),
            pl.BlockSpec((None, 1, d), lambda i: (layer, 0, 0)),
            resident((d, d_ff)),
            resident((d, d_ff)),
            resident((d_ff, d)),
        ],
        out_specs=pl.BlockSpec((tm, d), lambda i: (i, 0)),
        out_shape=jax.ShapeDtypeStruct((rows, d), F32),
        compiler_params=_cparams(("arbitrary",)),
        name="swiglu",
    )(x, mod, mod, mod, ln_g, ln_b, w_gate, w_up, w_down)


def _rope_layout(w):
    lead = w.shape[:-1]
    w = w.reshape(lead + (ATT_HEADS, 2, 2, 2, ROPE_FREQS))
    nd = len(lead)
    perm = tuple(range(nd)) + (nd, nd + 3, nd + 1, nd + 2, nd + 4)
    return w.transpose(perm).reshape(lead + (ATT_HEADS * 2 * QK_DIM,))


def _rope_tables(seq):
    rows = seq // GRID_W
    row = np.repeat(np.arange(rows), GRID_W).astype(np.float32)
    col = np.tile(np.arange(GRID_W), rows).astype(np.float32)
    inv = (ROPE_BASE ** (-np.arange(ROPE_FREQS, dtype=np.float32) / ROPE_FREQS)).astype(np.float32)
    ar = row[:, None] * inv[None, :]
    ac = col[:, None] * inv[None, :]
    ang = np.concatenate([ar, ac, ar, ac], axis=-1).astype(np.float32)
    cos = np.cos(ang).astype(np.float32)
    sin = np.sin(ang).astype(np.float32)
    cos = np.concatenate([cos, cos], axis=-1)
    sin = np.concatenate([-sin, sin], axis=-1)

    def with_identity(t, fill):
        ident = np.full((TM_IN, LANES), fill, np.float32)
        return jnp.asarray(np.concatenate([t, ident], axis=0), dtype=F32)

    return with_identity(cos, 1.0), with_identity(sin, 0.0)


def _dft_tables(length):
    def cos_sin(n):
        k = np.arange(n, dtype=np.int64)
        ang = 2.0 * np.pi * ((k[:, None] * k[None, :]) % n).astype(np.float64) / n
        return np.cos(ang), np.sin(ang)

    cc, sc = cos_sin(GROUP_C)
    cl, sl = cos_sin(length)
    cs = jnp.asarray(np.concatenate([cc, sc], axis=1), dtype=F32).astype(BF16)
    w_pos = jnp.asarray(np.concatenate([cl, -sl], axis=1), dtype=F32).astype(BF16)
    return cs, w_pos


def kernel(x, c, ctx, c_ctx, w_mod, b_mod, w_in, lam_qk, subln_g, w_att_br, w_four_br,
           w_pool_grp, pool_scale, w_pool_br, w_out, ln1_g, ln1_b, w_ffn_gate, w_ffn_up,
           w_ffn_down, ln2_g, ln2_b):
    n_batch, seq, d = x.shape
    ctx_len = ctx.shape[1]
    depth = w_mod.shape[0]
    assert d == D_MODEL and seq % ATT_TQ == 0 and seq % TM_IN == 0 and n_batch < MOD_ROWS
    assert (n_batch * ctx_len) % TM_IN == 0 and seq % ctx_len == 0
    alpha = (2 * depth) ** 0.25

    xs = jnp.concatenate([x.reshape(n_batch * seq, d), ctx.reshape(n_batch * ctx_len, d)], axis=0)

    cond = jnp.zeros((MOD_ROWS, d), F32).at[:n_batch].set(c).at[n_batch].set(c_ctx)
    mod = _mod_vectors(cond, w_mod, b_mod).reshape(depth * MOD_ROWS, 1, 6 * d)

    bf = lambda w: w.astype(BF16)
    w_att_b, w_four_b, w_pool_b, w_out_b = map(bf, (w_att_br, w_four_br, w_pool_br, w_out))
    w_grp_b, w_g_b, w_u_b, w_d_b = map(bf, (w_pool_grp, w_ffn_gate, w_ffn_up, w_ffn_down))
    w_in_b = bf(w_in)
    w_tiles = [w_in_b[..., j * TN_IN:(j + 1) * TN_IN] for j in range(N_IN // TN_IN)]
    for j in (OFF_K // TN_IN, OFF_Q // TN_IN):
        w_tiles[j] = _rope_layout(w_tiles[j])
    w_tiles = jnp.stack(w_tiles, axis=1)
    vec3 = lambda v: v.reshape(depth, 1, v.shape[-1])
    subln3, pscale3 = vec3(subln_g), vec3(pool_scale)
    ln1_g3, ln1_b3, ln2_g3, ln2_b3 = map(vec3, (ln1_g, ln1_b, ln2_g, ln2_b))

    rope = _rope_tables(seq)
    cs_lat, wpos_lat = _dft_tables(seq)
    cs_ctx, wpos_ctx = _dft_tables(ctx_len)

    n_lat_rows = n_batch * seq
    ctx_blk0 = n_lat_rows // ctx_len

    n_lat_tiles = n_lat_rows // TM_IN
    for l in range(depth):
        last = l == depth - 1
        lam_init = 0.8 - 0.6 * math.exp(-0.3 * l)
        proj = functools.partial(_in_projection, xs, mod, w_tiles, rope, l, n_lat_tiles,
                                 seq // TM_IN, n_batch)
        if last:
            pv = proj(n_row_tiles=n_lat_tiles)
            p, vt = proj(row_tile0=n_lat_tiles, n_row_tiles=xs.shape[0] // TM_IN - n_lat_tiles,
                         n_col_tiles=OFF_Q // TN_IN, prev=pv)
        else:
            p, vt = proj()
        att = _attention(p, vt, lam_qk, subln3, l, lam_init, n_batch, seq, ctx_len,
                         with_context=not last)
        four = _fourier(p, cs_lat, wpos_lat, seq, 0, n_batch)
        pool = _pool(p, w_grp_b, pscale3, l, seq, 0, n_batch)
        if not last:
            four = _fourier(p, cs_ctx, wpos_ctx, ctx_len, ctx_blk0, n_batch, prev=four)
            pool = _pool(p, w_grp_b, pscale3, l, ctx_len, ctx_blk0, n_batch, prev=pool)
        xs = _merge(att, four, pool, p, xs, mod, ln1_g3, ln1_b3, w_att_b, w_four_b, w_pool_b,
                    w_out_b, l, alpha, seq // TM_MIX, n_batch,
                    rows=n_lat_rows if last else xs.shape[0])
        xs = _ffn(xs, mod, ln2_g3, ln2_b3, w_g_b, w_u_b, w_d_b, l, alpha, seq // TM_FFN, n_batch)

    return xs.reshape(n_batch, seq, d)
```
